```python
import math
import jax, jax.numpy as jnp
from jax import lax
import numpy as np

D_MODEL = 4096
BATCH = 4
SEQ = 2048
DEPTH = 2
DEC_BATCH = 8
DEC_SEQ = 4
PAST_LEN = 16384
PAGE_SIZE = 128

MIX_W = D_MODEL
ATT_W = MIX_W // 2
RWKV_W = MIX_W - ATT_W
HEAD_DIM = 128
N_ATT_HEADS = ATT_W // HEAD_DIM
RWKV_HEAD = 64
N_RWKV_HEADS = RWKV_W // RWKV_HEAD
IN_COLS = 3 * ATT_W + 3 * RWKV_W
MOBA_BLOCK = 256
MOBA_TOPK = 3
Q_CHUNK = 32
ROPE_THETA = 10000.0
ATT_SCALE = HEAD_DIM ** -0.5
R_DECAY = max(32, int(round(D_MODEL ** 0.5 * 1.8 / 32)) * 32)
R_AAA = max(32, int(round(D_MODEL ** 0.5 * 1.8 / 32)) * 32)
R_GATE = max(32, int(round(D_MODEL ** 0.6 * 0.6 / 32)) * 32)
D_FF = (8 * D_MODEL + 3 * 256 - 1) // (3 * 256) * 256
RMS_EPS = 1e-6
GN_EPS = 64e-5

kernel_name = 'hymba_moba_rwkv7_step'


def rms_norm(x, g):
    xf = x.astype(jnp.float32)
    y = xf * lax.rsqrt(jnp.mean(xf * xf, axis=-1, keepdims=True) + RMS_EPS)
    return (y * g.astype(jnp.float32)).astype(x.dtype)


def rope(x, pos):
    half = HEAD_DIM // 2
    inv_freq = ROPE_THETA ** (-jnp.arange(half, dtype=jnp.float32) / half)
    ang = pos.astype(jnp.float32)[:, None] * inv_freq[None, :]
    cos = jnp.cos(ang)[:, None, :]
    sin = jnp.sin(ang)[:, None, :]
    xf = x.astype(jnp.float32)
    x1, x2 = xf[..., :half], xf[..., half:]
    return jnp.concatenate([x1 * cos - x2 * sin, x2 * cos + x1 * sin], axis=-1).astype(x.dtype)


def moba_query_chunk(q, pos, kb, vb, means):
    c = q.shape[0]
    nb = kb.shape[1]
    cur = pos // MOBA_BLOCK
    gate = jnp.einsum('chd,hnd->chn', q.astype(jnp.float32), means)
    past = jnp.arange(nb)[None, None, :] < cur[:, None, None]
    gate = jnp.where(past, gate, -jnp.inf)
    _, sel = lax.top_k(gate, MOBA_TOPK)
    sel_ok = jnp.arange(MOBA_TOPK)[None, :] < cur[:, None]
    own = jnp.broadcast_to(cur[:, None, None], (c, N_ATT_HEADS, 1))
    blocks = jnp.concatenate([sel, own], axis=-1)
    h_idx = jnp.arange(N_ATT_HEADS)[None, :, None]
    kg = kb[h_idx, blocks]
    vg = vb[h_idx, blocks]
    own_pos = cur[:, None] * MOBA_BLOCK + jnp.arange(MOBA_BLOCK)[None, :]
    own_ok = (own_pos <= pos[:, None])[:, None, None, :]
    ok = jnp.concatenate([
        jnp.broadcast_to(sel_ok[:, None, :, None], (c, N_ATT_HEADS, MOBA_TOPK, MOBA_BLOCK)),
        jnp.broadcast_to(own_ok, (c, N_ATT_HEADS, 1, MOBA_BLOCK))], axis=2)
    s = jnp.einsum('chd,chjkd->chjk', q, kg).astype(jnp.float32) * ATT_SCALE
    s = jnp.where(ok, s, -jnp.inf)
    p = jax.nn.softmax(s.reshape(c, N_ATT_HEADS, -1), axis=-1).reshape(s.shape)
    return jnp.einsum('chjk,chjkd->chd', p.astype(vg.dtype), vg)


def moba_sequence(q, pos, k, v):
    L = k.shape[0]
    nb = max(-(-L // MOBA_BLOCK), MOBA_TOPK)
    pad = nb * MOBA_BLOCK - L
    kb = jnp.pad(k, ((0, pad), (0, 0), (0, 0))).reshape(nb, MOBA_BLOCK, N_ATT_HEADS, HEAD_DIM).transpose(2, 0, 1, 3)
    vb = jnp.pad(v, ((0, pad), (0, 0), (0, 0))).reshape(nb, MOBA_BLOCK, N_ATT_HEADS, HEAD_DIM).transpose(2, 0, 1, 3)
    means = jnp.mean(kb.astype(jnp.float32), axis=2)
    t = q.shape[0]
    c = math.gcd(t, Q_CHUNK)
    qc = q.reshape(t // c, c, N_ATT_HEADS, HEAD_DIM)
    pc = pos.reshape(t // c, c)
    out = lax.map(lambda a: moba_query_chunk(a[0], a[1], kb, vb, means), (qc, pc))
    return out.reshape(t, N_ATT_HEADS, HEAD_DIM)


def wkv_scan(s0, r, w, k, v, a, b):
    def step(s, inp):
        r_t, w_t, k_t, v_t, a_t, b_t = inp
        sa = jnp.einsum('bhvk,bhk->bhv', s, a_t)
        s = s * w_t[:, :, None, :] + sa[..., None] * b_t[:, :, None, :] + v_t[..., None] * k_t[:, :, None, :]
        return s, jnp.einsum('bhvk,bhk->bhv', s, r_t)
    xs = tuple(jnp.moveaxis(z, 1, 0) for z in (r, w, k, v, a, b))
    s, o = lax.scan(step, s0.astype(jnp.float32), xs)
    return s, jnp.moveaxis(o, 0, 1)


def trunk_layer(x, pos, shift_prev, wkv_prev, attend, lw):
    f32 = jnp.float32
    B, T, _ = x.shape
    xn = rms_norm(x, lw['norm_mix'])
    y = jnp.einsum('btd,dc->btc', xn, lw['w_in'])
    q, k, v, y_rkv = jnp.split(y, [ATT_W, 2 * ATT_W, 3 * ATT_W], axis=-1)
    q = rope(q.reshape(B, T, N_ATT_HEADS, HEAD_DIM), pos)
    k = rope(k.reshape(B, T, N_ATT_HEADS, HEAD_DIM), pos)
    v = v.reshape(B, T, N_ATT_HEADS, HEAD_DIM)
    o_att = attend(q, k, v).reshape(B, T, ATT_W)
    shift_prev = shift_prev.astype(xn.dtype)
    dx = jnp.concatenate([shift_prev[:, None], xn[:, :-1]], axis=1) - xn
    y_prev = jnp.einsum('bd,dc->bc', shift_prev, lw['w_in'][:, 3 * ATT_W:])
    y_rkv_shift = jnp.concatenate([y_prev[:, None], y_rkv[:, :-1]], axis=1)
    rkv = y_rkv + (y_rkv_shift - y_rkv) * lw['mu_rkv'].reshape(-1)
    r, kr, vr = jnp.split(rkv, 3, axis=-1)
    mu = lw['mu_wag']
    xw = xn + dx * mu[0]
    xa = xn + dx * mu[1]
    xg = xn + dx * mu[2]
    w_log = -jax.nn.softplus(-(lw['w0'] + jnp.tanh(xw @ lw['w1']) @ lw['w2']).astype(f32)) - 0.5
    decay = jnp.exp(-jnp.exp(w_log))
    a = jax.nn.sigmoid((lw['a0'] + (xa @ lw['a1']) @ lw['a2']).astype(f32))
    g = jax.nn.sigmoid(xg @ lw['g1']) @ lw['g2']
    hs = lambda z: z.reshape(B, T, N_RWKV_HEADS, RWKV_HEAD).astype(f32)
    hp = lambda z: z.reshape(N_RWKV_HEADS, RWKV_HEAD).astype(f32)
    r_h, k_h, v_h, a_h, w_h = hs(r), hs(kr), hs(vr), hs(a), hs(decay)
    kk = k_h * hp(lw['k_k'])
    kk = kk / jnp.maximum(jnp.sqrt(jnp.sum(kk * kk, axis=-1, keepdims=True)), 1e-12)
    k_h = k_h * (1.0 + (a_h - 1.0) * hp(lw['k_a']))
    wkv_new, o = wkv_scan(wkv_prev, r_h, w_h, k_h, v_h, -kk, kk * a_h)
    o_mean = jnp.mean(o, axis=-1, keepdims=True)
    o_var = jnp.mean(jnp.square(o - o_mean), axis=-1, keepdims=True)
    o = (o - o_mean) * lax.rsqrt(o_var + GN_EPS) * hp(lw['ln_w']) + hp(lw['ln_b'])
    o = o + jnp.sum(r_h * k_h * lw['r_k'].astype(f32), axis=-1, keepdims=True) * v_h
    o_rwkv = o.reshape(B, T, RWKV_W).astype(x.dtype) * g
    h = x + jnp.concatenate([o_att, o_rwkv], axis=-1) @ lw['w_out']
    hn = rms_norm(h, lw['norm_ffn'])
    ff = (jax.nn.silu(hn @ lw['w_gate']) * (hn @ lw['w_up'])) @ lw['w_down']
    return h + ff, k, v, wkv_new.astype(x.dtype), xn[:, -1]


def setup_inputs(seed: int = 0) -> dict:
    key = jax.random.key(seed)
    ks = iter(jax.random.split(key, 40))
    f32 = jnp.float32
    nrm = lambda shape, scale: jax.random.normal(next(ks), shape, f32) * scale
    uni = lambda shape, lo, hi: jax.random.uniform(next(ks), shape, f32, minval=lo, maxval=hi)
    n_pages = PAST_LEN // PAGE_SIZE
    n_used = DEC_BATCH * n_pages
    n_pool = n_used + max(1, n_used // 4)
    inv_d = D_MODEL ** -0.5
    x_prompt = nrm((BATCH, SEQ, D_MODEL), 1.0)
    x_sample = nrm((DEC_BATCH, DEC_SEQ, D_MODEL), 1.0)
    cache_k = nrm((DEPTH, n_pool, PAGE_SIZE, N_ATT_HEADS, HEAD_DIM), 1.0)
    cache_v = nrm((DEPTH, n_pool, PAGE_SIZE, N_ATT_HEADS, HEAD_DIM), 1.0)
    state_wkv = nrm((DEPTH, DEC_BATCH, N_RWKV_HEADS, RWKV_HEAD, RWKV_HEAD), 0.1)
    state_shift = nrm((DEPTH, DEC_BATCH, D_MODEL), 1.0)
    page_table = jax.random.permutation(next(ks), n_pool)[:n_used].reshape(DEC_BATCH, n_pages).astype(jnp.int32)
    return {
        'x_prompt': x_prompt,
        'x_sample': x_sample,
        'cache_k': cache_k,
        'cache_v': cache_v,
        'state_wkv': state_wkv,
        'state_shift': state_shift,
        'page_table': page_table,
        'norm_mix': 1.0 + nrm((DEPTH, D_MODEL), 0.02),
        'norm_ffn': 1.0 + nrm((DEPTH, D_MODEL), 0.02),
        'norm_final': 1.0 + nrm((D_MODEL,), 0.02),
        'w_in': nrm((DEPTH, D_MODEL, IN_COLS), inv_d),
        'w_out': nrm((DEPTH, MIX_W, D_MODEL), MIX_W ** -0.5),
        'mu_rkv': uni((DEPTH, 3, RWKV_W), 0.0, 1.0),
        'mu_wag': uni((DEPTH, 3, D_MODEL), 0.0, 1.0),
        'decay_w0': uni((DEPTH, RWKV_W), -2.5, 0.5),
        'decay_w1': nrm((DEPTH, D_MODEL, R_DECAY), inv_d),
        'decay_w2': nrm((DEPTH, R_DECAY, RWKV_W), 0.1 * R_DECAY ** -0.5),
        'aaa_a0': nrm((DEPTH, RWKV_W), 0.1),
        'aaa_a1': nrm((DEPTH, D_MODEL, R_AAA), inv_d),
        'aaa_a2': nrm((DEPTH, R_AAA, RWKV_W), 0.1 * R_AAA ** -0.5),
        'gate_g1': nrm((DEPTH, D_MODEL, R_GATE), inv_d),
        'gate_g2': nrm((DEPTH, R_GATE, RWKV_W), R_GATE ** -0.5),
        'k_k': 0.85 + nrm((DEPTH, RWKV_W), 0.05),
        'k_a': 1.0 + nrm((DEPTH, RWKV_W), 0.05),
        'r_k': nrm((DEPTH, N_RWKV_HEADS, RWKV_HEAD), 0.1),
        'ln_x_w': 1.0 + nrm((DEPTH, RWKV_W), 0.02),
        'ln_x_b': nrm((DEPTH, RWKV_W), 0.02),
        'ffn_w_gate': nrm((DEPTH, D_MODEL, D_FF), inv_d),
        'ffn_w_up': nrm((DEPTH, D_MODEL, D_FF), inv_d),
        'ffn_w_down': nrm((DEPTH, D_FF, D_MODEL), D_FF ** -0.5),
    }


def reference(x_prompt, x_sample, cache_k, cache_v, state_wkv, state_shift, page_table,
              norm_mix, norm_ffn, norm_final, w_in, w_out, mu_rkv, mu_wag,
              decay_w0, decay_w1, decay_w2, aaa_a0, aaa_a1, aaa_a2, gate_g1, gate_g2,
              k_k, k_a, r_k, ln_x_w, ln_x_b, ffn_w_gate, ffn_w_up, ffn_w_down):
    pos_p = jnp.arange(SEQ, dtype=jnp.int32)
    pos_s = PAST_LEN + jnp.arange(DEC_SEQ, dtype=jnp.int32)

    def prompt_attend(q, k, v):
        return lax.map(lambda a: moba_sequence(a[0], pos_p, a[1], a[2]), (q, k, v))

    def make_sample_attend(layer):
        def sample_attend(q, k, v):
            def one(a):
                q_b, k_b, v_b, pages = a
                k_past = cache_k[layer, pages].reshape(-1, N_ATT_HEADS, HEAD_DIM).astype(k_b.dtype)
                v_past = cache_v[layer, pages].reshape(-1, N_ATT_HEADS, HEAD_DIM).astype(v_b.dtype)
                return moba_sequence(q_b, pos_s, jnp.concatenate([k_past, k_b], axis=0),
                                     jnp.concatenate([v_past, v_b], axis=0))
            return lax.map(one, (q, k, v, page_table))
        return sample_attend

    hp_, hs_ = x_prompt, x_sample
    kp_l, vp_l, wp_l, sp_l = [], [], [], []
    ks_l, vs_l, ws_l, ss_l = [], [], [], []
    for layer in range(DEPTH):
        lw = {'norm_mix': norm_mix[layer], 'norm_ffn': norm_ffn[layer], 'w_in': w_in[layer],
              'w_out': w_out[layer], 'mu_rkv': mu_rkv[layer], 'mu_wag': mu_wag[layer],
              'w0': decay_w0[layer], 'w1': decay_w1[layer], 'w2': decay_w2[layer],
              'a0': aaa_a0[layer], 'a1': aaa_a1[layer], 'a2': aaa_a2[layer],
              'g1': gate_g1[layer], 'g2': gate_g2[layer], 'k_k': k_k[layer], 'k_a': k_a[layer],
              'r_k': r_k[layer], 'ln_w': ln_x_w[layer], 'ln_b': ln_x_b[layer],
              'w_gate': ffn_w_gate[layer], 'w_up': ffn_w_up[layer], 'w_down': ffn_w_down[layer]}
        b_p = x_prompt.shape[0]
        hp_, kp, vp, wp, sp = trunk_layer(
            hp_, pos_p, jnp.zeros((b_p, D_MODEL), x_prompt.dtype),
            jnp.zeros((b_p, N_RWKV_HEADS, RWKV_HEAD, RWKV_HEAD), x_prompt.dtype), prompt_attend, lw)
        hs_, ks, vs, ws, ss = trunk_layer(
            hs_, pos_s, state_shift[layer], state_wkv[layer], make_sample_attend(layer), lw)
        kp_l.append(kp); vp_l.append(vp); wp_l.append(wp); sp_l.append(sp)
        ks_l.append(ks); vs_l.append(vs); ws_l.append(ws); ss_l.append(ss)
    y_prompt = rms_norm(hp_, norm_final)
    y_sample = rms_norm(hs_, norm_final)
    return (y_prompt, y_sample,
            jnp.stack(kp_l), jnp.stack(vp_l), jnp.stack(wp_l), jnp.stack(sp_l),
            jnp.stack(ks_l), jnp.stack(vs_l), jnp.stack(ws_l), jnp.stack(ss_l))
```

```python
import functools
import math

import jax
import jax.numpy as jnp
from jax import lax
from jax.experimental import pallas as pl
from jax.experimental.pallas import tpu as pltpu

MOBA_BLOCK = 256
MOBA_TOPK = 3
ROPE_THETA = 10000.0
RMS_EPS = 1e-6
GN_EPS = 64e-5
KK_EPS = 1e-12
DECAY_SCALE = math.exp(-0.5)

LANES = 128
SUBLANES = 8
WKV_CHUNK = 64
VMEM_LIMIT = 56 * 1024 * 1024

F32 = jnp.float32
BF16 = jnp.bfloat16
NEG_INF = float("-inf")


def _cparams(*sem):
    return pltpu.CompilerParams(dimension_semantics=sem, vmem_limit_bytes=VMEM_LIMIT)


def _round_up(n, m):
    return -(-n // m) * m


def _pick_tile(n, prefs):
    for p in prefs:
        if n % p == 0:
            return p
    return n


def _split3(x):
    h1 = x.astype(BF16)
    r1 = x - h1.astype(F32)
    h2 = r1.astype(BF16)
    h3 = (r1 - h2.astype(F32)).astype(BF16)
    return h1, h2, h3


def _dot(a, b):
    return jnp.dot(a, b, preferred_element_type=F32)


def _dot_nt(a, b):
    return lax.dot_general(a, b, (((1,), (1,)), ((), ())), preferred_element_type=F32)


def _dot_tn(a, b):
    return lax.dot_general(a, b, (((0,), (0,)), ((), ())), preferred_element_type=F32)


def _dot_exact_rhs(x, e):
    h1, h2, h3 = _split3(x)
    return _dot(h1, e) + _dot(h2, e) + _dot(h3, e)


def _dot_exact_lhs(e, x):
    h1, h2, h3 = _split3(x)
    return _dot(e, h1) + _dot(e, h2) + _dot(e, h3)


def _prologue_kernel(x_ref, sprev_ref, g_ref, mu_ref, w1_ref, a1_ref, g1_ref, w2_ref, a2_ref,
                     g2_ref, w0_ref, a0_ref, xn_ref, lw_ref, a_ref, gate_ref, shift_ref,
                     carry_ref, *, last_tile, last_row):
    i = pl.program_id(1)
    x = x_ref[0]
    tm = x.shape[0]
    xn = x * lax.rsqrt(jnp.mean(x * x, axis=-1, keepdims=True) + RMS_EPS) * g_ref[...]

    @pl.when(i == 0)
    def _():
        carry_ref[...] = sprev_ref[0]

    row = lax.broadcasted_iota(jnp.int32, xn.shape, 0)
    prev = jnp.where(row == 0, carry_ref[...], pltpu.roll(xn, 1, axis=0))
    carry_ref[...] = xn[tm - 1:tm, :]
    dx = prev - xn
    xw = (xn + dx * mu_ref[0:1, :]).astype(BF16)
    xa = (xn + dx * mu_ref[1:2, :]).astype(BF16)
    xg = (xn + dx * mu_ref[2:3, :]).astype(BF16)

    hw = jnp.tanh(_dot(xw, w1_ref[...])).astype(BF16)
    zw = w0_ref[...] + _dot(hw, w2_ref[...])
    lw_ref[0] = -DECAY_SCALE * jax.nn.sigmoid(zw)
    ha = _dot(xa, a1_ref[...]).astype(BF16)
    a_ref[0] = jax.nn.sigmoid(a0_ref[...] + _dot(ha, a2_ref[...]))
    hg = jax.nn.sigmoid(_dot(xg, g1_ref[...])).astype(BF16)
    gate_ref[0] = _dot(hg, g2_ref[...])
    xn_ref[0] = xn.astype(BF16)

    @pl.when(i == last_tile)
    def _():
        shift_ref[0] = xn[last_row:last_row + 1, :]


def _prologue(x, sprev, g, mu, w1, a1, g1, w2, a2, g2, w0, a0, t_real):
    B, T, D = x.shape
    RW = w2.shape[1]
    tm = _pick_tile(T, (128, 64, 32, 16, 8))
    nt = T // tm
    full = lambda arr: pl.BlockSpec(arr.shape, lambda b, i: (0,) * arr.ndim)
    row_spec = lambda w: pl.BlockSpec((1, tm, w), lambda b, i: (b, i, 0))
    kern = functools.partial(_prologue_kernel, last_tile=(t_real - 1) // tm,
                             last_row=(t_real - 1) % tm)
    return pl.pallas_call(
        kern,
        grid=(B, nt),
        in_specs=[row_spec(D), pl.BlockSpec((1, 1, D), lambda b, i: (b, 0, 0)), full(g), full(mu),
                  full(w1), full(a1), full(g1), full(w2), full(a2), full(g2), full(w0), full(a0)],
        out_specs=[row_spec(D), row_spec(RW), row_spec(RW), row_spec(RW),
                   pl.BlockSpec((1, 1, D), lambda b, i: (b, 0, 0))],
        out_shape=[jax.ShapeDtypeStruct((B, T, D), BF16), jax.ShapeDtypeStruct((B, T, RW), F32),
                   jax.ShapeDtypeStruct((B, T, RW), F32), jax.ShapeDtypeStruct((B, T, RW), F32),
                   jax.ShapeDtypeStruct((B, 1, D), F32)],
        scratch_shapes=[pltpu.VMEM((1, D), F32)],
        compiler_params=_cparams("arbitrary", "arbitrary"),
    )(x, sprev, g, mu, w1, a1, g1, w2, a2, g2, w0, a0)


def _rmsnorm_kernel(x_ref, g_ref, o_ref):
    x = x_ref[...]
    y = x * lax.rsqrt(jnp.mean(x * x, axis=-1, keepdims=True) + RMS_EPS) * g_ref[...]
    o_ref[...] = y.astype(o_ref.dtype)


def _rmsnorm(x2d, g, out_dtype):
    M, D = x2d.shape
    tm = _pick_tile(M, (256, 128, 64, 32, 16, 8))
    return pl.pallas_call(
        _rmsnorm_kernel,
        grid=(M // tm,),
        in_specs=[pl.BlockSpec((tm, D), lambda i: (i, 0)), pl.BlockSpec((1, D), lambda i: (0, 0))],
        out_specs=pl.BlockSpec((tm, D), lambda i: (i, 0)),
        out_shape=jax.ShapeDtypeStruct((M, D), out_dtype),
        compiler_params=_cparams("arbitrary"),
    )(x2d, g)


def _proj_kernel(x_ref, w_ref, o_ref):
    o_ref[...] = _dot(x_ref[...], w_ref[...])


def _proj_rope_kernel(x_ref, w_ref, cos_ref, sin_ref, o_ref, *, head_dim):
    y = _dot(x_ref[...], w_ref[...])
    cos = cos_ref[...]
    sin = sin_ref[...]
    for h in range(y.shape[1] // head_dim):
        yh = y[:, h * head_dim:(h + 1) * head_dim]
        o_ref[:, h * head_dim:(h + 1) * head_dim] = (
            yh * cos + pltpu.roll(yh, head_dim // 2, axis=1) * sin)


def _project(x2d, w_all, layer, col_off, n_cols, rope=None):
    M, K = x2d.shape
    row_prefs = (1024, 512, 256, 128, 64, 32, 16, 8)
    if rope is not None and M > row_prefs[0]:
        tm = _pick_tile(math.gcd(M, rope[3]), row_prefs)
    else:
        tm = _pick_tile(M, row_prefs)
    tn = _pick_tile(math.gcd(n_cols, col_off) if col_off else n_cols, (512, 256, 128))
    off = col_off // tn
    in_specs = [pl.BlockSpec((tm, K), lambda i, j: (i, 0)),
                pl.BlockSpec((None, K, tn), lambda i, j: (layer, 0, j + off))]
    args = [x2d, w_all]
    kern = _proj_kernel
    if rope is not None:
        cos, sin, head_dim, T = rope
        if tm <= T:
            per = T // tm
            tab_map = lambda i, j: (i % per, 0)
        else:
            cos = jnp.tile(cos, (tm // T, 1))
            sin = jnp.tile(sin, (tm // T, 1))
            tab_map = lambda i, j: (0, 0)
        in_specs += [pl.BlockSpec((tm, head_dim), tab_map), pl.BlockSpec((tm, head_dim), tab_map)]
        args += [cos, sin]
        kern = functools.partial(_proj_rope_kernel, head_dim=head_dim)
    return pl.pallas_call(
        kern,
        grid=(M // tm, n_cols // tn),
        in_specs=in_specs,
        out_specs=pl.BlockSpec((tm, tn), lambda i, j: (i, j)),
        out_shape=jax.ShapeDtypeStruct((M, n_cols), F32),
        compiler_params=_cparams("arbitrary", "arbitrary"),
    )(*args)


def _mm_res_kernel(*refs, n_lhs):
    res_ref = refs[2 * n_lhs]
    o_ref = refs[2 * n_lhs + 1]
    acc = res_ref[...]
    for t in range(n_lhs):
        acc = acc + _dot(refs[t][...], refs[n_lhs + t][...])
    o_ref[...] = acc


def _matmul_residual(lhs_list, w_all, layer, res2d):
    M, N = res2d.shape
    n_lhs = len(lhs_list)
    ks = [l.shape[1] for l in lhs_list]
    kblk = ks[0]
    assert all(k == kblk for k in ks)
    tm = _pick_tile(M, (512, 256, 128, 64, 32, 16, 8))
    tn = _pick_tile(N, (256, 128))
    in_specs = [pl.BlockSpec((tm, kblk), lambda i, j: (i, 0)) for _ in lhs_list]
    in_specs += [pl.BlockSpec((None, kblk, tn), functools.partial(lambda i, j, t: (layer, t, j), t=t))
                 for t in range(n_lhs)]
    in_specs += [pl.BlockSpec((tm, tn), lambda i, j: (i, j))]
    return pl.pallas_call(
        functools.partial(_mm_res_kernel, n_lhs=n_lhs),
        grid=(M // tm, N // tn),
        in_specs=in_specs,
        out_specs=pl.BlockSpec((tm, tn), lambda i, j: (i, j)),
        out_shape=jax.ShapeDtypeStruct((M, N), F32),
        compiler_params=_cparams("arbitrary", "arbitrary"),
    )(*lhs_list, *([w_all] * n_lhs), res2d)


def _swiglu_kernel(x_ref, wg_ref, wu_ref, o_ref):
    x = x_ref[...]
    g = _dot(x, wg_ref[...])
    u = _dot(x, wu_ref[...])
    o_ref[...] = (g * jax.nn.sigmoid(g) * u).astype(o_ref.dtype)


def _swiglu(x2d, wg_all, wu_all, layer):
    M, K = x2d.shape
    N = wg_all.shape[2]
    tm = _pick_tile(M, (1024, 512, 256, 128, 64, 32, 16, 8))
    tn = _pick_tile(N, (256, 128))
    wspec = pl.BlockSpec((None, K, tn), lambda i, j: (layer, 0, j))
    return pl.pallas_call(
        _swiglu_kernel,
        grid=(M // tm, N // tn),
        in_specs=[pl.BlockSpec((tm, K), lambda i, j: (i, 0)), wspec, wspec],
        out_specs=pl.BlockSpec((tm, tn), lambda i, j: (i, j)),
        out_shape=jax.ShapeDtypeStruct((M, N), BF16),
        compiler_params=_cparams("arbitrary", "arbitrary"),
    )(x2d, wg_all, wu_all)


def _rank_select(g, n_rows, row_of):
    row = lax.broadcasted_iota(jnp.int32, g.shape, 0)

    def body(m, cnt):
        gm = row_of(m)
        better = (gm > g) | ((gm == g) & (m < row))
        return cnt + better.astype(jnp.int32)

    cnt = lax.fori_loop(0, n_rows, body, jnp.zeros(g.shape, jnp.int32))
    return cnt < MOBA_TOPK


def _moba_prompt_kernel(q_ref, k_ref, v_ref, o_ref, means_ref, kb_ref, vt_ref, gate_ref, sel_ref,
                        *, nb, scale):
    qi = pl.program_id(2)
    blk = MOBA_BLOCK

    @pl.when(qi == 0)
    def _():
        means_ref[...] = jnp.zeros(means_ref.shape, F32)
        for n in range(nb):
            kn = k_ref[0, n * blk:(n + 1) * blk, :]
            means_ref[n:n + 1, :] = jnp.mean(kn, axis=0, keepdims=True)
            kb_ref[n] = kn.astype(BF16)
            vt_ref[n] = v_ref[0, n * blk:(n + 1) * blk, :].T.astype(BF16)

    q = q_ref[0]
    qb = q.astype(BF16)

    mh, ml, _ = _split3(means_ref[...])
    qh, ql, _ = _split3(q)
    gate = _dot_nt(mh, qh) + _dot_nt(mh, ql) + _dot_nt(ml, qh)
    row = lax.broadcasted_iota(jnp.int32, gate.shape, 0)
    past = row < qi
    gate_ref[...] = jnp.where(past, gate, NEG_INF)
    g = gate_ref[...]
    sel = past & _rank_select(g, nb, lambda m: gate_ref[pl.ds(m, 1), :])
    sel_f = sel.astype(F32)
    for n in range(nb):
        sel_ref[n] = sel_f[n:n + 1, :]

    s = _dot_nt(kb_ref[qi], qb) * scale
    kidx = lax.broadcasted_iota(jnp.int32, s.shape, 0)
    qidx = lax.broadcasted_iota(jnp.int32, s.shape, 1)
    s = jnp.where(kidx <= qidx, s, NEG_INF)
    m0 = jnp.max(s, axis=0, keepdims=True)
    p = jnp.exp(s - m0)
    l0 = jnp.sum(p, axis=0, keepdims=True)
    acc0 = _dot(vt_ref[qi], p.astype(BF16))

    def body(n, carry):
        m, l, acc = carry
        s = _dot_nt(kb_ref[n], qb) * scale
        s = jnp.where(sel_ref[n] > 0.0, s, NEG_INF)
        m_new = jnp.maximum(m, jnp.max(s, axis=0, keepdims=True))
        alpha = jnp.exp(m - m_new)
        p = jnp.exp(s - m_new)
        l = l * alpha + jnp.sum(p, axis=0, keepdims=True)
        acc = acc * alpha + _dot(vt_ref[n], p.astype(BF16))
        return m_new, l, acc

    _, l, acc = lax.fori_loop(0, qi, body, (m0, l0, acc0))
    o_ref[0] = (acc / l).T.astype(o_ref.dtype)


def _moba_prompt(q, k, v, head_dim):
    B, T, W = q.shape
    assert T % MOBA_BLOCK == 0 and head_dim == LANES
    H = W // head_dim
    nb = T // MOBA_BLOCK
    nbp = _round_up(nb, SUBLANES)
    kern = functools.partial(_moba_prompt_kernel, nb=nb, scale=head_dim ** -0.5)
    kv_spec = pl.BlockSpec((1, T, head_dim), lambda b, h, i: (b, 0, h))
    return pl.pallas_call(
        kern,
        grid=(B, H, nb),
        in_specs=[pl.BlockSpec((1, MOBA_BLOCK, head_dim), lambda b, h, i: (b, i, h)), kv_spec, kv_spec],
        out_specs=pl.BlockSpec((1, MOBA_BLOCK, head_dim), lambda b, h, i: (b, i, h)),
        out_shape=jax.ShapeDtypeStruct((B, T, W), BF16),
        scratch_shapes=[pltpu.VMEM((nbp, head_dim), F32),
                        pltpu.VMEM((nb, MOBA_BLOCK, head_dim), BF16),
                        pltpu.VMEM((nb, head_dim, MOBA_BLOCK), BF16),
                        pltpu.VMEM((nbp, MOBA_BLOCK), F32),
                        pltpu.VMEM((nbp, 1, MOBA_BLOCK), F32)],
        compiler_params=_cparams("arbitrary", "arbitrary", "arbitrary"),
    )(q, k, v)


def _moba_gate_kernel(pt_ref, q_ref, e_ref, dsel_ref, *rest, ppb, nbk):
    k_refs = rest[:ppb]
    sel_ref = rest[ppb]
    gate_ref = rest[ppb + 1]
    n = pl.program_id(1)
    tot = k_refs[0][...].sum(axis=0, keepdims=True)
    for r in k_refs[1:]:
        tot = tot + r[...].sum(axis=0, keepdims=True)
    mean = tot * (1.0 / MOBA_BLOCK)
    prod = q_ref[0] * mean
    per_head = _dot_exact_rhs(prod, e_ref[...])
    gate_ref[pl.ds(n, 1), :] = jnp.sum(per_head * dsel_ref[...], axis=0, keepdims=True)

    @pl.when(n == nbk - 1)
    def _():
        g = gate_ref[...]
        sel = _rank_select(g, nbk, lambda m: gate_ref[pl.ds(m, 1), :])
        sel_ref[0] = sel.astype(F32)


def _moba_sample_kernel(pt_ref, q_ref, ks_ref, vs_ref, sel_ref, lsel_ref, hmask_ref, hmask_t_ref,
                        *rest, ppb, nbk, scale, n_heads):
    k_refs = rest[:ppb]
    v_refs = rest[ppb:2 * ppb]
    o_ref = rest[2 * ppb]
    qbd_ref, m_ref, l_ref, acc_ref = rest[2 * ppb + 1:]
    n = pl.program_id(1)
    qp = q_ref.shape[1]
    lw = lsel_ref.shape[1]

    def col(x_row):
        return jnp.broadcast_to(x_row, (SUBLANES, lw)).T[:, 0:1]

    @pl.when(n == 0)
    def _():
        qb = q_ref[0].astype(BF16).astype(F32)
        qt = _dot_tn(qb, lsel_ref[0:qp, :])
        qbd = qt * hmask_ref[...]
        qbd_ref[...] = qbd.astype(BF16)
        ks = ks_ref[0].astype(BF16).astype(F32)
        s = _dot(ks, qbd) * scale
        kidx = lax.broadcasted_iota(jnp.int32, s.shape, 0)
        qidx = lax.broadcasted_iota(jnp.int32, s.shape, 1) // n_heads
        s = jnp.where(kidx <= qidx, s, NEG_INF)
        m0 = jnp.max(s, axis=0, keepdims=True)
        p = jnp.exp(s - m0)
        m_ref[...] = m0
        l_ref[...] = jnp.sum(p, axis=0, keepdims=True)
        acc_ref[...] = _dot_tn(p.astype(BF16).astype(F32), vs_ref[0].astype(BF16).astype(F32))

    @pl.when(n > 0)
    def _():
        kb = jnp.concatenate([r[...] for r in k_refs], axis=0).astype(BF16)
        vb = jnp.concatenate([r[...] for r in v_refs], axis=0).astype(BF16)
        s = _dot(kb, qbd_ref[...]) * scale
        s = jnp.where(sel_ref[0, pl.ds(n - 1, 1), :] > 0.0, s, NEG_INF)
        m = m_ref[...]
        m_new = jnp.maximum(m, jnp.max(s, axis=0, keepdims=True))
        alpha = jnp.exp(m - m_new)
        p = jnp.exp(s - m_new)
        m_ref[...] = m_new
        l_ref[...] = l_ref[...] * alpha + jnp.sum(p, axis=0, keepdims=True)
        acc_ref[...] = acc_ref[...] * col(alpha) + _dot(p.T.astype(BF16), vb)

    @pl.when(n == nbk)
    def _():
        o_full = acc_ref[...] / col(l_ref[...]) * hmask_t_ref[...]
        o_ref[0] = _dot_exact_lhs(lsel_ref[...].astype(BF16), o_full)[0:qp].astype(o_ref.dtype)


def _moba_sample(q, k_new, v_new, cache_k, cache_v, page_table, layer, head_dim):
    B, Qp, W = q.shape
    H = W // head_dim
    page = cache_k.shape[2]
    n_pages = page_table.shape[1]
    assert MOBA_BLOCK % page == 0 and (n_pages * page) % MOBA_BLOCK == 0 and Qp <= MOBA_BLOCK
    ppb = MOBA_BLOCK // page
    nbk = n_pages // ppb
    assert nbk >= MOBA_TOPK
    lw = _round_up(Qp * H, LANES)
    lane = jnp.arange(lw)
    colw = jnp.arange(W)
    live = lane < Qp * H
    e = ((colw[:, None] // head_dim) == (lane[None, :] % H)) & live[None, :]
    dsel = ((lane[None, :] // H) == jnp.arange(Qp)[:, None]) & live[None, :]
    e_bf = e.astype(BF16)
    dsel_f = dsel.astype(F32)
    dsel16 = _pad_rows(dsel_f, 2 * SUBLANES)
    hmask = e.astype(F32)
    hmask_t = hmask.T

    def page_spec(j, shift):
        def imap(b, n, pt):
            blk = jnp.maximum(n - shift, 0)
            return (layer, pt[b, blk * ppb + j], 0, 0)
        return pl.BlockSpec((None, None, page, W), imap)

    qspec = pl.BlockSpec((1, Qp, W), lambda b, n, pt: (b, 0, 0))
    const = lambda arr: pl.BlockSpec(arr.shape, lambda b, n, pt: (0,) * arr.ndim)

    sel = pl.pallas_call(
        functools.partial(_moba_gate_kernel, ppb=ppb, nbk=nbk),
        grid_spec=pltpu.PrefetchScalarGridSpec(
            num_scalar_prefetch=1,
            grid=(B, nbk),
            in_specs=[qspec, const(e_bf), const(dsel_f)] + [page_spec(j, 0) for j in range(ppb)],
            out_specs=pl.BlockSpec((1, nbk, lw), lambda b, n, pt: (b, 0, 0)),
            scratch_shapes=[pltpu.VMEM((nbk, lw), F32)]),
        out_shape=jax.ShapeDtypeStruct((B, nbk, lw), F32),
        compiler_params=_cparams("arbitrary", "arbitrary"),
    )(page_table, q, e_bf, dsel_f, *([cache_k] * ppb))

    return pl.pallas_call(
        functools.partial(_moba_sample_kernel, ppb=ppb, nbk=nbk, scale=head_dim ** -0.5, n_heads=H),
        grid_spec=pltpu.PrefetchScalarGridSpec(
            num_scalar_prefetch=1,
            grid=(B, nbk + 1),
            in_specs=[qspec, qspec, qspec,
                      pl.BlockSpec((1, nbk, lw), lambda b, n, pt: (b, 0, 0)),
                      const(dsel16), const(hmask), const(hmask_t)]
                     + [page_spec(j, 1) for j in range(ppb)] * 2,
            out_specs=pl.BlockSpec((1, Qp, W), lambda b, n, pt: (b, 0, 0)),
            scratch_shapes=[pltpu.VMEM((W, lw), BF16), pltpu.VMEM((1, lw), F32),
                            pltpu.VMEM((1, lw), F32), pltpu.VMEM((lw, W), F32)]),
        out_shape=jax.ShapeDtypeStruct((B, Qp, W), BF16),
        compiler_params=_cparams("arbitrary", "arbitrary"),
    )(page_table, q, k_new, v_new, sel, dsel16, hmask, hmask_t, *([cache_k] * ppb), *([cache_v] * ppb))


def _wkv_kernel(*refs, has_state, n_pairs, t_real, n_chunks, head):
    it = iter(refs)
    yr_ref, yk_ref, yv_ref = next(it), next(it), next(it)
    pr_ref, pk_ref, pv_ref = next(it), next(it), next(it)
    lw_ref, a_ref, g_ref = next(it), next(it), next(it)
    mu_ref, pvec_ref = next(it), next(it)
    tri_ref, seg_ref = next(it), next(it)
    s0_ref = next(it) if has_state else None
    o_ref, sout_ref = next(it), next(it)
    sv_ref, carry_ref = next(it), next(it)

    c = pl.program_id(2)
    C = yr_ref.shape[1]
    G = 2 * C
    f0 = (lax.broadcasted_iota(jnp.int32, (C, LANES), 1) < head)
    lane_g = lax.broadcasted_iota(jnp.int32, (G, LANES), 1)
    row_g = lax.broadcasted_iota(jnp.int32, (G, LANES), 0)
    stack_mask = (row_g // C) == (lane_g // head)
    rr = lax.broadcasted_iota(jnp.int32, (G, G), 0)
    cc = lax.broadcasted_iota(jnp.int32, (G, G), 1)
    same = (rr // C) == (cc // C)
    strict = same & ((cc % C) < (rr % C))
    incl = same & ((cc % C) <= (rr % C))
    pr_i = lax.broadcasted_iota(jnp.int32, (head, LANES), 0)
    pc_i = lax.broadcasted_iota(jnp.int32, (head, LANES), 1)
    place = [(pc_i == pr_i + h * head).astype(BF16) for h in range(2)]
    seg = seg_ref[...]
    tri = tri_ref[...]
    trow = lax.broadcasted_iota(jnp.int32, (C, LANES), 0)

    def stack(x):
        return jnp.where(stack_mask, jnp.concatenate([x, x], axis=0), 0.0)

    @pl.when(c == 0)
    def _():
        carry_ref[0:1, :] = pr_ref[0]
        carry_ref[1:2, :] = pk_ref[0]
        carry_ref[2:3, :] = pv_ref[0]
        for p in range(n_pairs):
            if has_state:
                blocks = [_dot_exact_rhs(s0_ref[0, 2 * p + h], place[h]) for h in range(2)]
                sv_ref[p] = jnp.concatenate(blocks, axis=0)
            else:
                sv_ref[p] = jnp.zeros((LANES, LANES), F32)

    for p in range(n_pairs):
        sl = slice(p * LANES, (p + 1) * LANES)

        def mixed(y_ref, idx):
            y = y_ref[0, :, sl]
            prev = jnp.where(trow == 0, carry_ref[idx:idx + 1, sl], pltpu.roll(y, 1, axis=0))
            carry_ref[idx:idx + 1, sl] = y[C - 1:C, :]
            return y + (prev - y) * mu_ref[idx:idx + 1, sl]

        r = mixed(yr_ref, 0)
        k = mixed(yk_ref, 1)
        v = mixed(yv_ref, 2)
        a = a_ref[0, :, sl]
        lw = lw_ref[0, :, sl]
        k_k = pvec_ref[0:1, sl]
        k_a = pvec_ref[1:2, sl]
        r_k = pvec_ref[2:3, sl]
        ln_w = pvec_ref[3:4, sl]
        ln_b = pvec_ref[4:5, sl]

        kk = k * k_k
        n2 = _dot_exact_rhs(kk * kk, seg)
        kk = kk / jnp.maximum(jnp.sqrt(n2), KK_EPS)
        k2 = k * (1.0 + (a - 1.0) * k_a)
        if t_real is not None:
            live = (c * C + trow) < t_real
            lw = jnp.where(live, lw, 0.0)
            kk = jnp.where(live, kk, 0.0)
            k2 = jnp.where(live, k2, 0.0)
            v = jnp.where(live, v, 0.0)
        av = -kk
        bv = kk * a

        logp = _dot_exact_lhs(tri, lw)
        logpc = logp[C - 1:C, :]
        inv_p = jnp.exp(-logp)
        a_t = av * jnp.exp(logp - lw)
        r_t = r * jnp.exp(logp)
        b_t = bv * inv_p
        k_t = k2 * inv_p
        tail = jnp.exp(logpc - logp)
        b_h = bv * tail
        k_h = k2 * tail

        lhs = jnp.concatenate([stack(a_t), stack(r_t)], axis=0).astype(BF16)
        rhs = jnp.concatenate([stack(b_t), stack(k_t)], axis=0).astype(BF16)
        vst = stack(v).astype(BF16)
        quad = _dot_nt(lhs, rhs)
        sv = sv_ref[p]
        ss = _dot_nt(lhs, sv.astype(BF16))
        n_pow = jnp.where(strict, quad[0:G, 0:G], 0.0)
        a_ak = jnp.where(strict, quad[0:G, G:2 * G], 0.0).astype(BF16)
        a_rb = jnp.where(incl, quad[G:2 * G, 0:G], 0.0)
        a_rk = jnp.where(incl, quad[G:2 * G, G:2 * G], 0.0)

        u = ss[0:G] + _dot(a_ak, vst)
        span = 1
        while span < C:
            nb16 = n_pow.astype(BF16)
            u = u + _dot(nb16, u.astype(BF16))
            span *= 2
            if span < C:
                n_pow = _dot(nb16, nb16)
        uv = jnp.concatenate([u.astype(BF16), vst], axis=0)
        o_st = ss[G:2 * G] + _dot(jnp.concatenate([a_rb, a_rk], axis=1).astype(BF16), uv)
        o = o_st[0:C] + o_st[C:G]
        bk = jnp.concatenate([stack(b_h), stack(k_h)], axis=0).astype(BF16)
        sv_new = sv * jnp.exp(logpc) + _dot_tn(uv, bk)
        sv_ref[p] = sv_new

        mean = _dot_exact_rhs(o, seg) * (1.0 / head)
        d = o - mean
        var = _dot_exact_rhs(d * d, seg) * (1.0 / head)
        y = d * lax.rsqrt(var + GN_EPS) * ln_w + ln_b
        y = y + _dot_exact_rhs(r * k2 * r_k, seg) * v
        o_ref[0, :, sl] = (y * g_ref[0, :, sl]).astype(o_ref.dtype)

        @pl.when(c == n_chunks - 1)
        def _():
            for h in range(2):
                rows = sv_new[h * head:(h + 1) * head, :]
                sout_ref[0, 2 * p + h] = _dot_nt(*_pair3(rows, place[h]))


def _pair3(x, e):
    h1, h2, h3 = _split3(x)
    return jnp.concatenate([h1, h2, h3], axis=1), jnp.concatenate([e, e, e], axis=1)


def _wkv(y_rkv, y_prev, lw, a, g, mu_rkv, pvec, state0, t_real):
    B, t_in, RW3 = y_rkv.shape
    RW = RW3 // 3
    head = LANES // 2
    NH = RW // head
    C = WKV_CHUNK
    T = _round_up(t_in, C)
    if T != t_in:
        pad_t = lambda z: jnp.pad(z, ((0, 0), (0, T - t_in), (0, 0)))
        y_rkv, lw, a, g = pad_t(y_rkv), pad_t(lw), pad_t(a), pad_t(g)
    n_chunks = T // C
    n_pairs = _pick_tile(RW // LANES, (4, 2, 1))
    lwd = n_pairs * LANES
    ncol = RW // lwd
    G = 2 * C
    tri = (jnp.arange(C)[:, None] >= jnp.arange(C)[None, :]).astype(BF16)
    seg = ((jnp.arange(LANES)[:, None] // head) == (jnp.arange(LANES)[None, :] // head)).astype(BF16)

    def cols(part):
        return pl.BlockSpec((1, C, lwd), lambda b, j, c: (b, c, j + part * ncol))

    def prev_cols(part):
        return pl.BlockSpec((1, 1, lwd), lambda b, j, c: (b, 0, j + part * ncol))

    tile = pl.BlockSpec((1, C, lwd), lambda b, j, c: (b, c, j))
    in_specs = [cols(0), cols(1), cols(2), prev_cols(0), prev_cols(1), prev_cols(2), tile, tile, tile,
                pl.BlockSpec((3, lwd), lambda b, j, c: (0, j)),
                pl.BlockSpec((SUBLANES, lwd), lambda b, j, c: (0, j)),
                pl.BlockSpec((C, C), lambda b, j, c: (0, 0)),
                pl.BlockSpec((LANES, LANES), lambda b, j, c: (0, 0))]
    args = [y_rkv, y_rkv, y_rkv, y_prev, y_prev, y_prev, lw, a, g, mu_rkv, pvec, tri, seg]
    state_spec = pl.BlockSpec((1, 2 * n_pairs, head, head), lambda b, j, c: (b, j, 0, 0))
    if state0 is not None:
        in_specs.append(state_spec)
        args.append(state0)
    kern = functools.partial(_wkv_kernel, has_state=state0 is not None, n_pairs=n_pairs,
                             t_real=None if t_real == T else t_real, n_chunks=n_chunks, head=head)
    o, state = pl.pallas_call(
        kern,
        grid=(B, ncol, n_chunks),
        in_specs=in_specs,
        out_specs=[tile, state_spec],
        out_shape=[jax.ShapeDtypeStruct((B, T, RW), BF16),
                   jax.ShapeDtypeStruct((B, NH, head, head), F32)],
        scratch_shapes=[pltpu.VMEM((n_pairs, LANES, LANES), F32), pltpu.VMEM((SUBLANES, lwd), F32)],
        compiler_params=_cparams("arbitrary", "arbitrary", "arbitrary"),
    )(*args)
    return o[:, :t_in], state


def _rope_tables(pos, head_dim):
    half = head_dim // 2
    inv_freq = ROPE_THETA ** (-jnp.arange(half, dtype=F32) / half)
    ang = pos.astype(F32)[:, None] * inv_freq[None, :]
    cos, sin = jnp.cos(ang), jnp.sin(ang)
    return jnp.concatenate([cos, cos], axis=1), jnp.concatenate([-sin, sin], axis=1)


def _pad_cols(w, mult):
    pad = _round_up(w.shape[-1], mult) - w.shape[-1]
    return jnp.pad(w, [(0, 0)] * (w.ndim - 1) + [(0, pad)]) if pad else w


def _pad_rows(w, mult):
    pad = _round_up(w.shape[-2], mult) - w.shape[-2]
    return jnp.pad(w, [(0, 0)] * (w.ndim - 2) + [(0, pad), (0, 0)]) if pad else w


def kernel(x_prompt, x_sample, cache_k, cache_v, state_wkv, state_shift, page_table, norm_mix, norm_ffn, norm_final, w_in, w_out, mu_rkv, mu_wag, decay_w0, decay_w1, decay_w2, aaa_a0, aaa_a1, aaa_a2, gate_g1, gate_g2, k_k, k_a, r_k, ln_x_w, ln_x_b, ffn_w_gate, ffn_w_up, ffn_w_down):
    depth = w_in.shape[0]
    D = x_prompt.shape[-1]
    n_att_heads, head_dim = cache_k.shape[3], cache_k.shape[4]
    att_w = n_att_heads * head_dim
    rw = mu_rkv.shape[-1]
    n_rwkv_heads, rwkv_head = r_k.shape[1], r_k.shape[2]
    assert rwkv_head * 2 == LANES and head_dim == LANES
    Bp, Tp, _ = x_prompt.shape
    Bs, Ts, _ = x_sample.shape
    Tsp = _round_up(Ts, SUBLANES)
    past_len = page_table.shape[1] * cache_k.shape[2]

    w_in_b, w_out_b = w_in.astype(BF16), w_out.astype(BF16)
    wg_b, wu_b, wd_b = ffn_w_gate.astype(BF16), ffn_w_up.astype(BF16), ffn_w_down.astype(BF16)
    w1_b, a1_b = _pad_cols(decay_w1, LANES).astype(BF16), _pad_cols(aaa_a1, LANES).astype(BF16)
    g1_b = _pad_cols(gate_g1, LANES).astype(BF16)
    w2_b, a2_b = _pad_rows(decay_w2, LANES).astype(BF16), _pad_rows(aaa_a2, LANES).astype(BF16)
    g2_b = _pad_rows(gate_g2, LANES).astype(BF16)
    cache_k2 = cache_k.reshape(cache_k.shape[:3] + (att_w,))
    cache_v2 = cache_v.reshape(cache_v.shape[:3] + (att_w,))

    cos_p, sin_p = _rope_tables(jnp.arange(Tp, dtype=jnp.int32), head_dim)
    cos_s, sin_s = _rope_tables(past_len + jnp.arange(Tsp, dtype=jnp.int32), head_dim)

    x_s = jnp.pad(x_sample, ((0, 0), (0, Tsp - Ts), (0, 0)))
    groups = [
        dict(x=x_prompt, B=Bp, T=Tp, t_real=Tp, cos=cos_p, sin=sin_p, prompt=True),
        dict(x=x_s, B=Bs, T=Tsp, t_real=Ts, cos=cos_s, sin=sin_s, prompt=False),
    ]
    outs = [dict(k=[], v=[], wkv=[], shift=[]) for _ in groups]

    for layer in range(depth):
        pvec = jnp.stack([k_k[layer], k_a[layer], r_k[layer].reshape(-1), ln_x_w[layer], ln_x_b[layer]]
                         + [jnp.zeros((rw,), F32)] * (SUBLANES - 5))
        for grp, out in zip(groups, outs):
            B, T, x = grp["B"], grp["T"], grp["x"]
            M = B * T
            if grp["prompt"]:
                sprev = jnp.zeros((B, 1, D), F32)
                state0 = None
            else:
                sprev = state_shift[layer][:, None, :]
                state0 = state_wkv[layer]
            xn, lw, a, g, shift = _prologue(
                x, sprev, norm_mix[layer][None], mu_wag[layer], w1_b[layer], a1_b[layer], g1_b[layer],
                w2_b[layer], a2_b[layer], g2_b[layer], decay_w0[layer][None], aaa_a0[layer][None],
                grp["t_real"])
            xn2 = xn.reshape(M, D)
            rope = (grp["cos"], grp["sin"], head_dim, T)
            q = _project(xn2, w_in_b, layer, 0, att_w, rope).reshape(B, T, att_w)
            k = _project(xn2, w_in_b, layer, att_w, att_w, rope).reshape(B, T, att_w)
            v = _project(xn2, w_in_b, layer, 2 * att_w, att_w).reshape(B, T, att_w)
            y_rkv = _project(xn2, w_in_b, layer, 3 * att_w, 3 * rw).reshape(B, T, 3 * rw)
            if grp["prompt"]:
                y_prev = jnp.zeros((B, 1, 3 * rw), F32)
                o_att = _moba_prompt(q, k, v, head_dim)
            else:
                sp_b = _pad_rows(sprev.reshape(B, D), SUBLANES).astype(BF16)
                y_prev = _project(sp_b, w_in_b, layer, 3 * att_w, 3 * rw)[:B].reshape(B, 1, 3 * rw)
                o_att = _moba_sample(q, k, v, cache_k2, cache_v2, page_table, layer, head_dim)
            o_rwkv, wkv_new = _wkv(y_rkv, y_prev, lw, a, g, mu_rkv[layer], pvec, state0, grp["t_real"])
            h = _matmul_residual([o_att.reshape(M, att_w), o_rwkv.reshape(M, rw)], w_out_b, layer,
                                 x.reshape(M, D))
            hn = _rmsnorm(h, norm_ffn[layer][None], BF16)
            act = _swiglu(hn, wg_b, wu_b, layer)
            x_next = _matmul_residual([act], wd_b, layer, h)
            grp["x"] = x_next.reshape(B, T, D)
            tr = grp["t_real"]
            out["k"].append(k[:, :tr].reshape(B, tr, n_att_heads, head_dim))
            out["v"].append(v[:, :tr].reshape(B, tr, n_att_heads, head_dim))
            out["wkv"].append(wkv_new)
            out["shift"].append(shift.reshape(B, D))

    ys = []
    for grp in groups:
        B, T = grp["B"], grp["T"]
        y = _rmsnorm(grp["x"].reshape(B * T, D), norm_final[None], F32).reshape(B, T, D)
        ys.append(y[:, :grp["t_real"]])
    po, so = outs
    return (ys[0], ys[1],
            jnp.stack(po["k"]), jnp.stack(po["v"]), jnp.stack(po["wkv"]), jnp.stack(po["shift"]),
            jnp.stack(so["k"]), jnp.stack(so["v"]), jnp.stack(so["wkv"]), jnp.stack(so["shift"]))
```

```python
import functools
import math

import jax
import jax.numpy as jnp
from jax import lax
from jax.experimental import pallas as pl
from jax.experimental.pallas import tpu as pltpu

MOBA_BLOCK = 256
MOBA_TOPK = 3
ROPE_THETA = 10000.0
RMS_EPS = 1e-6
GN_EPS = 64e-5
KK_EPS = 1e-12
DECAY_SCALE = math.exp(-0.5)

LANES = 128
SUBLANES = 8
WKV_CHUNK = 64
VMEM_LIMIT = 56 * 1024 * 1024

F32 = jnp.float32
BF16 = jnp.bfloat16
NEG_INF = float("-inf")


def _cparams(*sem):
    return pltpu.CompilerParams(dimension_semantics=sem, vmem_limit_bytes=VMEM_LIMIT)


def _round_up(n, m):
    return -(-n // m) * m


def _pick_tile(n, prefs):
    for p in prefs:
        if n % p == 0:
            return p
    return n


def _split3(x):
    h1 = x.astype(BF16)
    r1 = x - h1.astype(F32)
    h2 = r1.astype(BF16)
    h3 = (r1 - h2.astype(F32)).astype(BF16)
    return h1, h2, h3


def _dot(a, b):
    return jnp.dot(a, b, preferred_element_type=F32)


def _dot_nt(a, b):
    return lax.dot_general(a, b, (((1,), (1,)), ((), ())), preferred_element_type=F32)


def _dot_tn(a, b):
    return lax.dot_general(a, b, (((0,), (0,)), ((), ())), preferred_element_type=F32)


def _dot_exact_rhs(x, e):
    m = x.shape[0]
    y = _dot(jnp.concatenate(_split3(x), axis=0), e)
    return y[0:m] + y[m:2 * m] + y[2 * m:3 * m]


def _dot_exact_lhs(e, x):
    n = x.shape[1]
    y = _dot(e, jnp.concatenate(_split3(x), axis=1))
    return y[:, 0:n] + y[:, n:2 * n] + y[:, 2 * n:3 * n]


def _prologue_kernel(x_ref, sprev_ref, g_ref, mu_ref, w1_ref, a1_ref, g1_ref, w2_ref, a2_ref,
                     g2_ref, w0_ref, a0_ref, xn_ref, lw_ref, a_ref, gate_ref, shift_ref,
                     carry_ref, *, last_tile, last_row):
    i = pl.program_id(1)
    x = x_ref[0]
    tm = x.shape[0]
    xn = x * lax.rsqrt(jnp.mean(x * x, axis=-1, keepdims=True) + RMS_EPS) * g_ref[...]

    @pl.when(i == 0)
    def _():
        carry_ref[...] = sprev_ref[0]

    row = lax.broadcasted_iota(jnp.int32, xn.shape, 0)
    prev = jnp.where(row == 0, carry_ref[...], pltpu.roll(xn, 1, axis=0))
    carry_ref[...] = xn[tm - 1:tm, :]
    dx = prev - xn
    xw = (xn + dx * mu_ref[0:1, :]).astype(BF16)
    xa = (xn + dx * mu_ref[1:2, :]).astype(BF16)
    xg = (xn + dx * mu_ref[2:3, :]).astype(BF16)

    hw = jnp.tanh(_dot(xw, w1_ref[...])).astype(BF16)
    zw = w0_ref[...] + _dot(hw, w2_ref[...])
    lw_ref[0] = -DECAY_SCALE * jax.nn.sigmoid(zw)
    ha = _dot(xa, a1_ref[...]).astype(BF16)
    a_ref[0] = jax.nn.sigmoid(a0_ref[...] + _dot(ha, a2_ref[...]))
    hg = jax.nn.sigmoid(_dot(xg, g1_ref[...])).astype(BF16)
    gate_ref[0] = _dot(hg, g2_ref[...])
    xn_ref[0] = xn.astype(BF16)

    @pl.when(i == last_tile)
    def _():
        shift_ref[0] = xn[last_row:last_row + 1, :]


def _prologue(x, sprev, g, mu, w1, a1, g1, w2, a2, g2, w0, a0, t_real):
    B, T, D = x.shape
    RW = w2.shape[1]
    tm = _pick_tile(T, (128, 64, 32, 16, 8))
    nt = T // tm
    full = lambda arr: pl.BlockSpec(arr.shape, lambda b, i: (0,) * arr.ndim)
    row_spec = lambda w: pl.BlockSpec((1, tm, w), lambda b, i: (b, i, 0))
    kern = functools.partial(_prologue_kernel, last_tile=(t_real - 1) // tm,
                             last_row=(t_real - 1) % tm)
    return pl.pallas_call(
        kern,
        grid=(B, nt),
        in_specs=[row_spec(D), pl.BlockSpec((1, 1, D), lambda b, i: (b, 0, 0)), full(g), full(mu),
                  full(w1), full(a1), full(g1), full(w2), full(a2), full(g2), full(w0), full(a0)],
        out_specs=[row_spec(D), row_spec(RW), row_spec(RW), row_spec(RW),
                   pl.BlockSpec((1, 1, D), lambda b, i: (b, 0, 0))],
        out_shape=[jax.ShapeDtypeStruct((B, T, D), BF16), jax.ShapeDtypeStruct((B, T, RW), F32),
                   jax.ShapeDtypeStruct((B, T, RW), F32), jax.ShapeDtypeStruct((B, T, RW), F32),
                   jax.ShapeDtypeStruct((B, 1, D), F32)],
        scratch_shapes=[pltpu.VMEM((1, D), F32)],
        compiler_params=_cparams("arbitrary", "arbitrary"),
    )(x, sprev, g, mu, w1, a1, g1, w2, a2, g2, w0, a0)


def _rmsnorm_kernel(x_ref, g_ref, o_ref):
    x = x_ref[...]
    y = x * lax.rsqrt(jnp.mean(x * x, axis=-1, keepdims=True) + RMS_EPS) * g_ref[...]
    o_ref[...] = y.astype(o_ref.dtype)


def _rmsnorm(x2d, g, out_dtype):
    M, D = x2d.shape
    tm = _pick_tile(M, (256, 128, 64, 32, 16, 8))
    return pl.pallas_call(
        _rmsnorm_kernel,
        grid=(M // tm,),
        in_specs=[pl.BlockSpec((tm, D), lambda i: (i, 0)), pl.BlockSpec((1, D), lambda i: (0, 0))],
        out_specs=pl.BlockSpec((tm, D), lambda i: (i, 0)),
        out_shape=jax.ShapeDtypeStruct((M, D), out_dtype),
        compiler_params=_cparams("arbitrary"),
    )(x2d, g)


def _proj_kernel(x_ref, w_ref, o_ref):
    o_ref[...] = _dot(x_ref[...], w_ref[...])


def _proj_rope_kernel(x_ref, w_ref, cos_ref, sin_ref, o_ref, *, head_dim):
    y = _dot(x_ref[...], w_ref[...])
    cos = cos_ref[...]
    sin = sin_ref[...]
    for h in range(y.shape[1] // head_dim):
        yh = y[:, h * head_dim:(h + 1) * head_dim]
        o_ref[:, h * head_dim:(h + 1) * head_dim] = (
            yh * cos + pltpu.roll(yh, head_dim // 2, axis=1) * sin)


def _project(x2d, w_all, layer, col_off, n_cols, rope=None):
    M, K = x2d.shape
    row_prefs = (1024, 512, 256, 128, 64, 32, 16, 8)
    if rope is not None and M > row_prefs[0]:
        tm = _pick_tile(math.gcd(M, rope[3]), row_prefs)
    else:
        tm = _pick_tile(M, row_prefs)
    tn = _pick_tile(math.gcd(n_cols, col_off) if col_off else n_cols, (512, 256, 128))
    off = col_off // tn
    in_specs = [pl.BlockSpec((tm, K), lambda i, j: (i, 0)),
                pl.BlockSpec((None, K, tn), lambda i, j: (layer, 0, j + off))]
    args = [x2d, w_all]
    kern = _proj_kernel
    if rope is not None:
        cos, sin, head_dim, T = rope
        if tm <= T:
            per = T // tm
            tab_map = lambda i, j: (i % per, 0)
        else:
            cos = jnp.tile(cos, (tm // T, 1))
            sin = jnp.tile(sin, (tm // T, 1))
            tab_map = lambda i, j: (0, 0)
        in_specs += [pl.BlockSpec((tm, head_dim), tab_map), pl.BlockSpec((tm, head_dim), tab_map)]
        args += [cos, sin]
        kern = functools.partial(_proj_rope_kernel, head_dim=head_dim)
    return pl.pallas_call(
        kern,
        grid=(M // tm, n_cols // tn),
        in_specs=in_specs,
        out_specs=pl.BlockSpec((tm, tn), lambda i, j: (i, j)),
        out_shape=jax.ShapeDtypeStruct((M, n_cols), F32),
        compiler_params=_cparams("arbitrary", "arbitrary"),
    )(*args)


def _mm_res_kernel(*refs, n_lhs):
    res_ref = refs[2 * n_lhs]
    o_ref = refs[2 * n_lhs + 1]
    acc = res_ref[...]
    for t in range(n_lhs):
        acc = acc + _dot(refs[t][...], refs[n_lhs + t][...])
    o_ref[...] = acc


def _matmul_residual(lhs_list, w_all, layer, res2d):
    M, N = res2d.shape
    n_lhs = len(lhs_list)
    ks = [l.shape[1] for l in lhs_list]
    kblk = ks[0]
    assert all(k == kblk for k in ks)
    for tm_pref, tn_pref in ((1024, 512), (512, 512), (512, 256), (256, 256), (256, 128)):
        tm = _pick_tile(M, tuple(p for p in (1024, 512, 256, 128, 64, 32, 16, 8) if p <= tm_pref))
        tn = _pick_tile(N, tuple(p for p in (512, 256, 128) if p <= tn_pref))
        if 2 * (n_lhs * kblk * (tm + tn) * 2 + 2 * tm * tn * 4) <= VMEM_LIMIT * 3 // 4:
            break
    in_specs = [pl.BlockSpec((tm, kblk), lambda i, j: (i, 0)) for _ in lhs_list]
    in_specs += [pl.BlockSpec((None, kblk, tn), functools.partial(lambda i, j, t: (layer, t, j), t=t))
                 for t in range(n_lhs)]
    in_specs += [pl.BlockSpec((tm, tn), lambda i, j: (i, j))]
    return pl.pallas_call(
        functools.partial(_mm_res_kernel, n_lhs=n_lhs),
        grid=(M // tm, N // tn),
        in_specs=in_specs,
        out_specs=pl.BlockSpec((tm, tn), lambda i, j: (i, j)),
        out_shape=jax.ShapeDtypeStruct((M, N), F32),
        compiler_params=_cparams("arbitrary", "arbitrary"),
    )(*lhs_list, *([w_all] * n_lhs), res2d)


def _swiglu_kernel(x_ref, wg_ref, wu_ref, o_ref):
    x = x_ref[...]
    g = _dot(x, wg_ref[...])
    u = _dot(x, wu_ref[...])
    o_ref[...] = (g * jax.nn.sigmoid(g) * u).astype(o_ref.dtype)


def _swiglu(x2d, wg_all, wu_all, layer):
    M, K = x2d.shape
    N = wg_all.shape[2]
    tm = _pick_tile(M, (1024, 512, 256, 128, 64, 32, 16, 8))
    tn = _pick_tile(N, (256, 128))
    wspec = pl.BlockSpec((None, K, tn), lambda i, j: (layer, 0, j))
    return pl.pallas_call(
        _swiglu_kernel,
        grid=(M // tm, N // tn),
        in_specs=[pl.BlockSpec((tm, K), lambda i, j: (i, 0)), wspec, wspec],
        out_specs=pl.BlockSpec((tm, tn), lambda i, j: (i, j)),
        out_shape=jax.ShapeDtypeStruct((M, N), BF16),
        compiler_params=_cparams("arbitrary", "arbitrary"),
    )(x2d, wg_all, wu_all)


def _rank_select(g, n_rows, row_of):
    row = lax.broadcasted_iota(jnp.int32, g.shape, 0)

    def body(m, cnt):
        gm = row_of(m)
        better = (gm > g) | ((gm == g) & (m < row))
        return cnt + better.astype(jnp.int32)

    cnt = lax.fori_loop(0, n_rows, body, jnp.zeros(g.shape, jnp.int32))
    return cnt < MOBA_TOPK


def _moba_prompt_kernel(q_ref, k_ref, v_ref, o_ref, means_ref, kb_ref, vt_ref, gate_ref, sel_ref,
                        *, nb, scale, n_heads, head_dim):
    qi = pl.program_id(2)
    blk = MOBA_BLOCK
    heads = range(n_heads)
    cols = [slice(h * head_dim, (h + 1) * head_dim) for h in heads]

    @pl.when(qi == 0)
    def _():
        means_ref[...] = jnp.zeros(means_ref.shape, F32)
        for h in heads:
            for n in range(nb):
                kn = k_ref[0, n * blk:(n + 1) * blk, cols[h]]
                means_ref[h, n:n + 1, :] = jnp.mean(kn, axis=0, keepdims=True)
                kb_ref[h, n] = kn.astype(BF16)
                vt_ref[h, n] = v_ref[0, n * blk:(n + 1) * blk, cols[h]].T.astype(BF16)

    q = [q_ref[0, :, cols[h]] for h in heads]
    qb = [z.astype(BF16) for z in q]

    for h in heads:
        mh, ml, _ = _split3(means_ref[h])
        qh, ql, _ = _split3(q[h])
        gate = _dot_nt(mh, qh) + _dot_nt(mh, ql) + _dot_nt(ml, qh)
        row = lax.broadcasted_iota(jnp.int32, gate.shape, 0)
        gate_ref[h] = jnp.where(row < qi, gate, NEG_INF)
    for h in heads:
        g = gate_ref[h]
        row = lax.broadcasted_iota(jnp.int32, g.shape, 0)
        sel = (row < qi) & _rank_select(g, nb, lambda m, h=h: gate_ref[h, pl.ds(m, 1), :])
        sel_f = sel.astype(F32)
        for n in range(nb):
            sel_ref[h, n] = sel_f[n:n + 1, :]

    s = [_dot_nt(kb_ref[h, qi], qb[h]) * scale for h in heads]
    kidx = lax.broadcasted_iota(jnp.int32, s[0].shape, 0)
    qidx = lax.broadcasted_iota(jnp.int32, s[0].shape, 1)
    s = [jnp.where(kidx <= qidx, z, NEG_INF) for z in s]
    m0 = [jnp.max(z, axis=0, keepdims=True) for z in s]
    p = [jnp.exp(s[h] - m0[h]) for h in heads]
    l0 = [jnp.sum(z, axis=0, keepdims=True) for z in p]
    acc0 = [_dot(vt_ref[h, qi], p[h].astype(BF16)) for h in heads]

    def body(n, carry):
        m, l, acc = carry
        s = [_dot_nt(kb_ref[h, n], qb[h]) * scale for h in heads]
        s = [jnp.where(sel_ref[h, n] > 0.0, s[h], NEG_INF) for h in heads]
        m_new = [jnp.maximum(m[h], jnp.max(s[h], axis=0, keepdims=True)) for h in heads]
        alpha = [jnp.exp(m[h] - m_new[h]) for h in heads]
        p = [jnp.exp(s[h] - m_new[h]) for h in heads]
        l = [l[h] * alpha[h] + jnp.sum(p[h], axis=0, keepdims=True) for h in heads]
        pv = [_dot(vt_ref[h, n], p[h].astype(BF16)) for h in heads]
        acc = [acc[h] * alpha[h] + pv[h] for h in heads]
        return m_new, l, acc

    _, l, acc = lax.fori_loop(0, qi, body, (m0, l0, acc0))
    for h in heads:
        o_ref[0, :, cols[h]] = (acc[h] / l[h]).T.astype(o_ref.dtype)


MOBA_HEADS_PER_STEP = 4


def _moba_prompt(q, k, v, head_dim):
    B, T, W = q.shape
    assert T % MOBA_BLOCK == 0 and head_dim == LANES
    H = W // head_dim
    hp = _pick_tile(H, (MOBA_HEADS_PER_STEP, 1))
    wp = hp * head_dim
    nb = T // MOBA_BLOCK
    nbp = _round_up(nb, SUBLANES)
    kern = functools.partial(_moba_prompt_kernel, nb=nb, scale=head_dim ** -0.5, n_heads=hp,
                             head_dim=head_dim)
    kv_spec = pl.BlockSpec((1, T, wp), lambda b, h, i: (b, 0, h))
    return pl.pallas_call(
        kern,
        grid=(B, H // hp, nb),
        in_specs=[pl.BlockSpec((1, MOBA_BLOCK, wp), lambda b, h, i: (b, i, h)), kv_spec, kv_spec],
        out_specs=pl.BlockSpec((1, MOBA_BLOCK, wp), lambda b, h, i: (b, i, h)),
        out_shape=jax.ShapeDtypeStruct((B, T, W), BF16),
        scratch_shapes=[pltpu.VMEM((hp, nbp, head_dim), F32),
                        pltpu.VMEM((hp, nb, MOBA_BLOCK, head_dim), BF16),
                        pltpu.VMEM((hp, nb, head_dim, MOBA_BLOCK), BF16),
                        pltpu.VMEM((hp, nbp, MOBA_BLOCK), F32),
                        pltpu.VMEM((hp, nbp, 1, MOBA_BLOCK), F32)],
        compiler_params=_cparams("arbitrary", "arbitrary", "arbitrary"),
    )(q, k, v)


def _head_rows(page_refs, h, n_heads):
    rows = page_refs[0].shape[0] // n_heads
    return jnp.concatenate([r[pl.ds(h, rows, stride=n_heads), :] for r in page_refs], axis=0)


def _moba_scores_kernel(pt_ref, q_ref, e_ref, dsel_ref, *rest, ppb, nbk, n_heads, scale):
    k_refs = rest[:ppb]
    scores_ref, sel_ref, gate_ref = rest[ppb:ppb + 3]
    n = pl.program_id(1)
    qf = q_ref[0]
    dh = qf.shape[1] // n_heads
    heads = range(n_heads)
    kh = [_head_rows(k_refs, h, n_heads) for h in heads]
    means = [jnp.sum(kh[h], axis=0, keepdims=True) * (1.0 / MOBA_BLOCK) for h in heads]
    for h in heads:
        qh = qf[:, h * dh:(h + 1) * dh].astype(BF16)
        scores_ref[0, 0, h] = _dot_nt(qh, kh[h].astype(BF16)) * scale
    prod = qf * jnp.concatenate(means, axis=1)
    per_head = _dot_exact_rhs(prod, e_ref[...])
    gate_ref[pl.ds(n, 1), :] = jnp.sum(per_head * dsel_ref[...], axis=0, keepdims=True)

    @pl.when(n == nbk - 1)
    def _():
        g = gate_ref[...]
        sel = _rank_select(g, nbk, lambda m: gate_ref[pl.ds(m, 1), :])
        sel_ref[0] = sel.astype(F32)


def _moba_pv_kernel(pt_ref, q_ref, ko_ref, vo_ref, scores_ref, sel_ref, dselh_ref, *rest,
                    ppb, nbk, n_heads, scale):
    v_refs = rest[:ppb]
    o_ref = rest[ppb]
    m_ref, l_ref, acc_ref = rest[ppb + 1:]
    n = pl.program_id(1)
    qp = q_ref.shape[1]
    dh = q_ref.shape[2] // n_heads

    @pl.when(n == 0)
    def _():
        qf = q_ref[0]
        for h in range(n_heads):
            sl = slice(h * dh, (h + 1) * dh)
            s = _dot_nt(qf[:, sl].astype(BF16), ko_ref[0, :, sl].astype(BF16)) * scale
            kidx = lax.broadcasted_iota(jnp.int32, s.shape, 1)
            qidx = lax.broadcasted_iota(jnp.int32, s.shape, 0)
            s = jnp.where(kidx <= qidx, s, NEG_INF)
            m = jnp.max(s, axis=1, keepdims=True)
            p = jnp.exp(s - m)
            m_ref[h] = jnp.broadcast_to(m, (qp, LANES))
            l_ref[h] = jnp.broadcast_to(jnp.sum(p, axis=1, keepdims=True), (qp, LANES))
            acc_ref[h] = _dot(p.astype(BF16), vo_ref[0, :, sl].astype(BF16))

    @pl.when(n > 0)
    def _():
        heads = range(n_heads)
        selrow = sel_ref[0, pl.ds(n - 1, 1), :]
        picked = [jnp.sum(selrow * dselh_ref[h], axis=1, keepdims=True) > 0.0 for h in heads]
        s = [jnp.where(picked[h], scores_ref[0, 0, h], NEG_INF) for h in heads]
        m_new = [jnp.maximum(m_ref[h], jnp.max(s[h], axis=1, keepdims=True)) for h in heads]
        alpha = [jnp.exp(m_ref[h] - m_new[h]) for h in heads]
        p = [jnp.exp(s[h] - m_new[h][:, 0:1]) for h in heads]
        pv = [_dot(p[h].astype(BF16), _head_rows(v_refs, h, n_heads).astype(BF16)) for h in heads]
        for h in heads:
            m_ref[h] = m_new[h]
            l_ref[h] = l_ref[h] * alpha[h] + jnp.sum(p[h], axis=1, keepdims=True)
            acc_ref[h] = acc_ref[h] * alpha[h][:, 0:1] + pv[h]

    @pl.when(n == nbk)
    def _():
        for h in range(n_heads):
            o_ref[0, :, h * dh:(h + 1) * dh] = (acc_ref[h] / l_ref[h][:, 0:1]).astype(o_ref.dtype)


def _moba_sample(q, k_new, v_new, cache_k, cache_v, page_table, layer):
    B, Ts, W = q.shape
    page, H, dh = cache_k.shape[2:]
    n_pages = page_table.shape[1]
    assert MOBA_BLOCK % page == 0 and (n_pages * page) % MOBA_BLOCK == 0 and dh == LANES
    ppb = MOBA_BLOCK // page
    nbk = n_pages // ppb
    assert nbk >= MOBA_TOPK
    qp = _round_up(Ts, 2 * SUBLANES)
    own = LANES
    assert qp <= own
    pad_rows = lambda z, r: jnp.pad(z, ((0, 0), (0, r - z.shape[1]), (0, 0)))
    q, k_new, v_new = pad_rows(q, qp), pad_rows(k_new, own), pad_rows(v_new, own)
    lw = _round_up(qp * H, LANES)
    lane = jnp.arange(lw)
    colw = jnp.arange(W)
    live = lane < qp * H
    e = (((colw[:, None] // dh) == (lane[None, :] % H)) & live[None, :]).astype(BF16)
    dsel = (((lane[None, :] // H) == jnp.arange(qp)[:, None]) & live[None, :]).astype(F32)
    dselh = (lane[None, None, :] == (jnp.arange(qp)[None, :, None] * H
                                      + jnp.arange(H)[:, None, None])).astype(F32)
    scale = dh ** -0.5

    def page_spec(j, shift):
        def imap(b, n, pt):
            blk = jnp.maximum(n - shift, 0)
            return (layer, pt[b, blk * ppb + j], 0, 0)
        return pl.BlockSpec((None, None, page * H, dh), imap)

    cache_k = cache_k.reshape(cache_k.shape[:2] + (page * H, dh))
    cache_v = cache_v.reshape(cache_v.shape[:2] + (page * H, dh))

    qspec = pl.BlockSpec((1, qp, W), lambda b, n, pt: (b, 0, 0))
    own_spec = pl.BlockSpec((1, own, W), lambda b, n, pt: (b, 0, 0))
    const = lambda arr: pl.BlockSpec(arr.shape, lambda b, n, pt: (0,) * arr.ndim)
    sel_spec = pl.BlockSpec((1, nbk, lw), lambda b, n, pt: (b, 0, 0))

    scores, sel = pl.pallas_call(
        functools.partial(_moba_scores_kernel, ppb=ppb, nbk=nbk, n_heads=H, scale=scale),
        grid_spec=pltpu.PrefetchScalarGridSpec(
            num_scalar_prefetch=1,
            grid=(B, nbk),
            in_specs=[qspec, const(e), const(dsel)] + [page_spec(j, 0) for j in range(ppb)],
            out_specs=[pl.BlockSpec((1, 1, H, qp, MOBA_BLOCK), lambda b, n, pt: (b, n, 0, 0, 0)), sel_spec],
            scratch_shapes=[pltpu.VMEM((nbk, lw), F32)]),
        out_shape=[jax.ShapeDtypeStruct((B, nbk, H, qp, MOBA_BLOCK), F32),
                   jax.ShapeDtypeStruct((B, nbk, lw), F32)],
        compiler_params=_cparams("arbitrary", "arbitrary"),
    )(page_table, q, e, dsel, *([cache_k] * ppb))

    o = pl.pallas_call(
        functools.partial(_moba_pv_kernel, ppb=ppb, nbk=nbk, n_heads=H, scale=scale),
        grid_spec=pltpu.PrefetchScalarGridSpec(
            num_scalar_prefetch=1,
            grid=(B, nbk + 1),
            in_specs=[qspec, own_spec, own_spec,
                      pl.BlockSpec((1, 1, H, qp, MOBA_BLOCK),
                                   lambda b, n, pt: (b, jnp.maximum(n - 1, 0), 0, 0, 0)),
                      sel_spec, const(dselh)] + [page_spec(j, 1) for j in range(ppb)],
            out_specs=qspec,
            scratch_shapes=[pltpu.VMEM((H, qp, LANES), F32), pltpu.VMEM((H, qp, LANES), F32),
                            pltpu.VMEM((H, qp, dh), F32)]),
        out_shape=jax.ShapeDtypeStruct((B, qp, W), BF16),
        compiler_params=_cparams("arbitrary", "arbitrary"),
    )(page_table, q, k_new, v_new, scores, sel, dselh, *([cache_v] * ppb))
    return o[:, :Ts]


def _wkv_kernel(*refs, has_state, n_pairs, t_real, n_chunks, head):
    it = iter(refs)
    yr_ref, yk_ref, yv_ref = next(it), next(it), next(it)
    pr_ref, pk_ref, pv_ref = next(it), next(it), next(it)
    lw_ref, a_ref, g_ref = next(it), next(it), next(it)
    mu_ref, pvec_ref = next(it), next(it)
    tri_ref, seg_ref = next(it), next(it)
    s0_ref = next(it) if has_state else None
    o_ref, sout_ref = next(it), next(it)
    sv_ref, carry_ref = next(it), next(it)

    c = pl.program_id(2)
    C = yr_ref.shape[1]
    G = 2 * C
    f0 = (lax.broadcasted_iota(jnp.int32, (C, LANES), 1) < head)
    lane_g = lax.broadcasted_iota(jnp.int32, (G, LANES), 1)
    row_g = lax.broadcasted_iota(jnp.int32, (G, LANES), 0)
    stack_mask = (row_g // C) == (lane_g // head)
    rr = lax.broadcasted_iota(jnp.int32, (G, G), 0)
    cc = lax.broadcasted_iota(jnp.int32, (G, G), 1)
    same = (rr // C) == (cc // C)
    strict = same & ((cc % C) < (rr % C))
    incl = same & ((cc % C) <= (rr % C))
    pr_i = lax.broadcasted_iota(jnp.int32, (head, LANES), 0)
    pc_i = lax.broadcasted_iota(jnp.int32, (head, LANES), 1)
    place = [(pc_i == pr_i + h * head).astype(BF16) for h in range(2)]
    seg = seg_ref[...]
    tri = tri_ref[...]
    trow = lax.broadcasted_iota(jnp.int32, (C, LANES), 0)

    def stack(x):
        return jnp.where(stack_mask, jnp.concatenate([x, x], axis=0), 0.0)

    @pl.when(c == 0)
    def _():
        carry_ref[0:1, :] = pr_ref[0]
        carry_ref[1:2, :] = pk_ref[0]
        carry_ref[2:3, :] = pv_ref[0]
        for p in range(n_pairs):
            if has_state:
                blocks = [_dot_exact_rhs(s0_ref[0, 2 * p + h], place[h]) for h in range(2)]
                sv_ref[p] = jnp.concatenate(blocks, axis=0)
            else:
                sv_ref[p] = jnp.zeros((LANES, LANES), F32)

    pairs = range(n_pairs)
    lanes_of = [slice(p * LANES, (p + 1) * LANES) for p in pairs]

    def mixed(y_ref, idx, sl):
        y = y_ref[0, :, sl]
        prev = jnp.where(trow == 0, carry_ref[idx:idx + 1, sl], pltpu.roll(y, 1, axis=0))
        carry_ref[idx:idx + 1, sl] = y[C - 1:C, :]
        return y + (prev - y) * mu_ref[idx:idx + 1, sl]

    r = [mixed(yr_ref, 0, sl) for sl in lanes_of]
    k = [mixed(yk_ref, 1, sl) for sl in lanes_of]
    v = [mixed(yv_ref, 2, sl) for sl in lanes_of]
    a = [a_ref[0, :, sl] for sl in lanes_of]
    lw = [lw_ref[0, :, sl] for sl in lanes_of]
    kk = [k[p] * pvec_ref[0:1, lanes_of[p]] for p in pairs]
    k2 = [k[p] * (1.0 + (a[p] - 1.0) * pvec_ref[1:2, lanes_of[p]]) for p in pairs]
    sums = [_dot_exact_rhs(jnp.concatenate(
        [kk[p] * kk[p], r[p] * k2[p] * pvec_ref[2:3, lanes_of[p]]], axis=0), seg) for p in pairs]
    kk = [kk[p] / jnp.maximum(jnp.sqrt(sums[p][0:C]), KK_EPS) for p in pairs]
    bonus = [sums[p][C:2 * C] for p in pairs]
    if t_real is not None:
        live = (c * C + trow) < t_real
        lw = [jnp.where(live, z, 0.0) for z in lw]
        kk = [jnp.where(live, z, 0.0) for z in kk]
        k2 = [jnp.where(live, z, 0.0) for z in k2]
        v = [jnp.where(live, z, 0.0) for z in v]
    bv = [kk[p] * a[p] for p in pairs]

    logp = [_dot_exact_lhs(tri, lw[p]) for p in pairs]
    logpc = [z[C - 1:C, :] for z in logp]
    lhs, rhs, vst, bk = [], [], [], []
    for p in pairs:
        inv_p = jnp.exp(-logp[p])
        tail = jnp.exp(logpc[p] - logp[p])
        a_t = -kk[p] * jnp.exp(logp[p] - lw[p])
        r_t = r[p] * jnp.exp(logp[p])
        lhs.append(jnp.concatenate([stack(a_t), stack(r_t)], axis=0).astype(BF16))
        rhs.append(jnp.concatenate([stack(bv[p] * inv_p), stack(k2[p] * inv_p)], axis=0).astype(BF16))
        bk.append(jnp.concatenate([stack(bv[p] * tail), stack(k2[p] * tail)], axis=0).astype(BF16))
        vst.append(stack(v[p]).astype(BF16))

    quad = [_dot_nt(lhs[p], rhs[p]) for p in pairs]
    sv = [sv_ref[p] for p in pairs]
    ss = [_dot_nt(lhs[p], sv[p].astype(BF16)) for p in pairs]
    n_pow = [jnp.where(strict, quad[p][0:G, 0:G], 0.0).astype(BF16) for p in pairs]
    a_ak = [jnp.where(strict, quad[p][0:G, G:2 * G], 0.0).astype(BF16) for p in pairs]
    a_r = [jnp.concatenate([jnp.where(incl, quad[p][G:2 * G, 0:G], 0.0),
                            jnp.where(incl, quad[p][G:2 * G, G:2 * G], 0.0)], axis=1).astype(BF16)
           for p in pairs]

    u = [ss[p][0:G] + _dot(a_ak[p], vst[p]) for p in pairs]
    span = 1
    while span < C:
        u = [u[p] + _dot(n_pow[p], u[p].astype(BF16)) for p in pairs]
        span *= 2
        if span < C:
            n_pow = [_dot(n_pow[p], n_pow[p]).astype(BF16) for p in pairs]
    uv = [jnp.concatenate([u[p].astype(BF16), vst[p]], axis=0) for p in pairs]
    o_st = [ss[p][G:2 * G] + _dot(a_r[p], uv[p]) for p in pairs]
    for p in pairs:
        sv_ref[p] = sv[p] * jnp.exp(logpc[p]) + _dot_tn(uv[p], bk[p])

    o = [z[0:C] + z[C:G] for z in o_st]
    d = [o[p] - _dot_exact_rhs(o[p], seg) * (1.0 / head) for p in pairs]
    var = [_dot_exact_rhs(d[p] * d[p], seg) * (1.0 / head) for p in pairs]
    for p in pairs:
        sl = lanes_of[p]
        y = d[p] * lax.rsqrt(var[p] + GN_EPS) * pvec_ref[3:4, sl] + pvec_ref[4:5, sl]
        y = y + bonus[p] * v[p]
        o_ref[0, :, sl] = (y * g_ref[0, :, sl]).astype(o_ref.dtype)

    @pl.when(c == n_chunks - 1)
    def _():
        for p in range(n_pairs):
            for h in range(2):
                rows = sv_ref[p, h * head:(h + 1) * head, :]
                sout_ref[0, 2 * p + h] = _dot_nt(*_pair3(rows, place[h]))


def _pair3(x, e):
    h1, h2, h3 = _split3(x)
    return jnp.concatenate([h1, h2, h3], axis=1), jnp.concatenate([e, e, e], axis=1)


def _wkv(y_rkv, y_prev, lw, a, g, mu_rkv, pvec, state0, t_real):
    B, t_in, RW3 = y_rkv.shape
    RW = RW3 // 3
    head = LANES // 2
    NH = RW // head
    C = WKV_CHUNK
    T = _round_up(t_in, C)
    if T != t_in:
        pad_t = lambda z: jnp.pad(z, ((0, 0), (0, T - t_in), (0, 0)))
        y_rkv, lw, a, g = pad_t(y_rkv), pad_t(lw), pad_t(a), pad_t(g)
    n_chunks = T // C
    n_pairs = _pick_tile(RW // LANES, (8, 4, 2, 1))
    lwd = n_pairs * LANES
    ncol = RW // lwd
    G = 2 * C
    tri = (jnp.arange(C)[:, None] >= jnp.arange(C)[None, :]).astype(BF16)
    seg = ((jnp.arange(LANES)[:, None] // head) == (jnp.arange(LANES)[None, :] // head)).astype(BF16)

    def cols(part):
        return pl.BlockSpec((1, C, lwd), lambda b, j, c: (b, c, j + part * ncol))

    def prev_cols(part):
        return pl.BlockSpec((1, 1, lwd), lambda b, j, c: (b, 0, j + part * ncol))

    tile = pl.BlockSpec((1, C, lwd), lambda b, j, c: (b, c, j))
    in_specs = [cols(0), cols(1), cols(2), prev_cols(0), prev_cols(1), prev_cols(2), tile, tile, tile,
                pl.BlockSpec((3, lwd), lambda b, j, c: (0, j)),
                pl.BlockSpec((SUBLANES, lwd), lambda b, j, c: (0, j)),
                pl.BlockSpec((C, C), lambda b, j, c: (0, 0)),
                pl.BlockSpec((LANES, LANES), lambda b, j, c: (0, 0))]
    args = [y_rkv, y_rkv, y_rkv, y_prev, y_prev, y_prev, lw, a, g, mu_rkv, pvec, tri, seg]
    state_spec = pl.BlockSpec((1, 2 * n_pairs, head, head), lambda b, j, c: (b, j, 0, 0))
    if state0 is not None:
        in_specs.append(state_spec)
        args.append(state0)
    kern = functools.partial(_wkv_kernel, has_state=state0 is not None, n_pairs=n_pairs,
                             t_real=None if t_real == T else t_real, n_chunks=n_chunks, head=head)
    o, state = pl.pallas_call(
        kern,
        grid=(B, ncol, n_chunks),
        in_specs=in_specs,
        out_specs=[tile, state_spec],
        out_shape=[jax.ShapeDtypeStruct((B, T, RW), BF16),
                   jax.ShapeDtypeStruct((B, NH, head, head), F32)],
        scratch_shapes=[pltpu.VMEM((n_pairs, LANES, LANES), F32), pltpu.VMEM((SUBLANES, lwd), F32)],
        compiler_params=_cparams("arbitrary", "arbitrary", "arbitrary"),
    )(*args)
    return o[:, :t_in], state


def _rope_tables(pos, head_dim):
    half = head_dim // 2
    inv_freq = ROPE_THETA ** (-jnp.arange(half, dtype=F32) / half)
    ang = pos.astype(F32)[:, None] * inv_freq[None, :]
    cos, sin = jnp.cos(ang), jnp.sin(ang)
    return jnp.concatenate([cos, cos], axis=1), jnp.concatenate([-sin, sin], axis=1)


def _pad_cols(w, mult):
    pad = _round_up(w.shape[-1], mult) - w.shape[-1]
    return jnp.pad(w, [(0, 0)] * (w.ndim - 1) + [(0, pad)]) if pad else w


def _pad_rows(w, mult):
    pad = _round_up(w.shape[-2], mult) - w.shape[-2]
    return jnp.pad(w, [(0, 0)] * (w.ndim - 2) + [(0, pad), (0, 0)]) if pad else w


def kernel(x_prompt, x_sample, cache_k, cache_v, state_wkv, state_shift, page_table, norm_mix, norm_ffn, norm_final, w_in, w_out, mu_rkv, mu_wag, decay_w0, decay_w1, decay_w2, aaa_a0, aaa_a1, aaa_a2, gate_g1, gate_g2, k_k, k_a, r_k, ln_x_w, ln_x_b, ffn_w_gate, ffn_w_up, ffn_w_down):
    depth = w_in.shape[0]
    D = x_prompt.shape[-1]
    n_att_heads, head_dim = cache_k.shape[3], cache_k.shape[4]
    att_w = n_att_heads * head_dim
    rw = mu_rkv.shape[-1]
    n_rwkv_heads, rwkv_head = r_k.shape[1], r_k.shape[2]
    assert rwkv_head * 2 == LANES and head_dim == LANES
    Bp, Tp, _ = x_prompt.shape
    Bs, Ts, _ = x_sample.shape
    Tsp = _round_up(Ts, SUBLANES)
    past_len = page_table.shape[1] * cache_k.shape[2]

    w_in_b, w_out_b = w_in.astype(BF16), w_out.astype(BF16)
    wg_b, wu_b, wd_b = ffn_w_gate.astype(BF16), ffn_w_up.astype(BF16), ffn_w_down.astype(BF16)
    w1_b, a1_b = _pad_cols(decay_w1, LANES).astype(BF16), _pad_cols(aaa_a1, LANES).astype(BF16)
    g1_b = _pad_cols(gate_g1, LANES).astype(BF16)
    w2_b, a2_b = _pad_rows(decay_w2, LANES).astype(BF16), _pad_rows(aaa_a2, LANES).astype(BF16)
    g2_b = _pad_rows(gate_g2, LANES).astype(BF16)

    cos_p, sin_p = _rope_tables(jnp.arange(Tp, dtype=jnp.int32), head_dim)
    cos_s, sin_s = _rope_tables(past_len + jnp.arange(Tsp, dtype=jnp.int32), head_dim)

    x_s = jnp.pad(x_sample, ((0, 0), (0, Tsp - Ts), (0, 0)))
    groups = [
        dict(x=x_prompt, B=Bp, T=Tp, t_real=Tp, cos=cos_p, sin=sin_p, prompt=True),
        dict(x=x_s, B=Bs, T=Tsp, t_real=Ts, cos=cos_s, sin=sin_s, prompt=False),
    ]
    outs = [dict(k=[], v=[], wkv=[], shift=[]) for _ in groups]

    for layer in range(depth):
        pvec = jnp.stack([k_k[layer], k_a[layer], r_k[layer].reshape(-1), ln_x_w[layer], ln_x_b[layer]]
                         + [jnp.zeros((rw,), F32)] * (SUBLANES - 5))
        for grp, out in zip(groups, outs):
            B, T, x = grp["B"], grp["T"], grp["x"]
            M = B * T
            if grp["prompt"]:
                sprev = jnp.zeros((B, 1, D), F32)
                state0 = None
            else:
                sprev = state_shift[layer][:, None, :]
                state0 = state_wkv[layer]
            xn, lw, a, g, shift = _prologue(
                x, sprev, norm_mix[layer][None], mu_wag[layer], w1_b[layer], a1_b[layer], g1_b[layer],
                w2_b[layer], a2_b[layer], g2_b[layer], decay_w0[layer][None], aaa_a0[layer][None],
                grp["t_real"])
            xn2 = xn.reshape(M, D)
            rope = (grp["cos"], grp["sin"], head_dim, T)
            q = _project(xn2, w_in_b, layer, 0, att_w, rope).reshape(B, T, att_w)
            k = _project(xn2, w_in_b, layer, att_w, att_w, rope).reshape(B, T, att_w)
            v = _project(xn2, w_in_b, layer, 2 * att_w, att_w).reshape(B, T, att_w)
            y_rkv = _project(xn2, w_in_b, layer, 3 * att_w, 3 * rw).reshape(B, T, 3 * rw)
            if grp["prompt"]:
                y_prev = jnp.zeros((B, 1, 3 * rw), F32)
                o_att = _moba_prompt(q, k, v, head_dim)
            else:
                sp_b = _pad_rows(sprev.reshape(B, D), SUBLANES).astype(BF16)
                y_prev = _project(sp_b, w_in_b, layer, 3 * att_w, 3 * rw)[:B].reshape(B, 1, 3 * rw)
                o_att = _moba_sample(q, k, v, cache_k, cache_v, page_table, layer)
            o_rwkv, wkv_new = _wkv(y_rkv, y_prev, lw, a, g, mu_rkv[layer], pvec, state0, grp["t_real"])
            h = _matmul_residual([o_att.reshape(M, att_w), o_rwkv.reshape(M, rw)], w_out_b, layer,
                                 x.reshape(M, D))
            hn = _rmsnorm(h, norm_ffn[layer][None], BF16)
            act = _swiglu(hn, wg_b, wu_b, layer)
            x_next = _matmul_residual([act], wd_b, layer, h)
            grp["x"] = x_next.reshape(B, T, D)
            tr = grp["t_real"]
            out["k"].append(k[:, :tr].reshape(B, tr, n_att_heads, head_dim))
            out["v"].append(v[:, :tr].reshape(B, tr, n_att_heads, head_dim))
            out["wkv"].append(wkv_new)
            out["shift"].append(shift.reshape(B, D))

    ys = []
    for grp in groups:
        B, T = grp["B"], grp["T"]
        y = _rmsnorm(grp["x"].reshape(B * T, D), norm_final[None], F32).reshape(B, T, D)
        ys.append(y[:, :grp["t_real"]])
    po, so = outs
    return (ys[0], ys[1],
            jnp.stack(po["k"]), jnp.stack(po["v"]), jnp.stack(po["wkv"]), jnp.stack(po["shift"]),
            jnp.stack(so["k"]), jnp.stack(so["v"]), jnp.stack(so["wkv"]), jnp.stack(so["shift"]))
```

```python
import functools
import math

import jax
import jax.numpy as jnp
from jax import lax
from jax.experimental import pallas as pl
from jax.experimental.pallas import tpu as pltpu

MOBA_BLOCK = 256
MOBA_TOPK = 3
ROPE_THETA = 10000.0
RMS_EPS = 1e-6
GN_EPS = 64e-5
KK_EPS = 1e-12
DECAY_SCALE = math.exp(-0.5)

LANES = 128
SUBLANES = 8
WKV_CHUNK = 64
WKV_SUM_TERMS = 2
VMEM_LIMIT = 56 * 1024 * 1024

F32 = jnp.float32
BF16 = jnp.bfloat16
NEG_INF = float("-inf")


def _cparams(*sem):
    return pltpu.CompilerParams(dimension_semantics=sem, vmem_limit_bytes=VMEM_LIMIT)


def _round_up(n, m):
    return -(-n // m) * m


def _pick_tile(n, prefs):
    for p in prefs:
        if n % p == 0:
            return p
    return n


def _split3(x):
    h1 = x.astype(BF16)
    r1 = x - h1.astype(F32)
    h2 = r1.astype(BF16)
    h3 = (r1 - h2.astype(F32)).astype(BF16)
    return h1, h2, h3


def _dot(a, b):
    return jnp.dot(a, b, preferred_element_type=F32)


def _dot_nt(a, b):
    return lax.dot_general(a, b, (((1,), (1,)), ((), ())), preferred_element_type=F32)


def _dot_tn(a, b):
    return lax.dot_general(a, b, (((0,), (0,)), ((), ())), preferred_element_type=F32)


def _dot_exact_rhs(x, e, terms=3):
    m = x.shape[0]
    y = _dot(jnp.concatenate(_split3(x)[:terms], axis=0), e)
    return sum(y[i * m:(i + 1) * m] for i in range(1, terms)) + y[0:m]


def _dot_exact_lhs(e, x, terms=3):
    n = x.shape[1]
    y = _dot(e, jnp.concatenate(_split3(x)[:terms], axis=1))
    return sum(y[:, i * n:(i + 1) * n] for i in range(1, terms)) + y[:, 0:n]


def _prologue_kernel(x_ref, sprev_ref, g_ref, mu_ref, w1_ref, a1_ref, g1_ref, w2_ref, a2_ref,
                     g2_ref, w0_ref, a0_ref, xn_ref, lw_ref, a_ref, gate_ref, shift_ref,
                     carry_ref, *, last_tile, last_row):
    i = pl.program_id(1)
    x = x_ref[0]
    tm = x.shape[0]
    xn = x * lax.rsqrt(jnp.mean(x * x, axis=-1, keepdims=True) + RMS_EPS) * g_ref[...]

    @pl.when(i == 0)
    def _():
        carry_ref[...] = sprev_ref[0]

    row = lax.broadcasted_iota(jnp.int32, xn.shape, 0)
    prev = jnp.where(row == 0, carry_ref[...], pltpu.roll(xn, 1, axis=0))
    carry_ref[...] = xn[tm - 1:tm, :]
    dx = prev - xn
    xw = (xn + dx * mu_ref[0:1, :]).astype(BF16)
    xa = (xn + dx * mu_ref[1:2, :]).astype(BF16)
    xg = (xn + dx * mu_ref[2:3, :]).astype(BF16)

    hw = jnp.tanh(_dot(xw, w1_ref[...])).astype(BF16)
    zw = w0_ref[...] + _dot(hw, w2_ref[...])
    lw_ref[0] = -DECAY_SCALE * jax.nn.sigmoid(zw)
    ha = _dot(xa, a1_ref[...]).astype(BF16)
    a_ref[0] = jax.nn.sigmoid(a0_ref[...] + _dot(ha, a2_ref[...]))
    hg = jax.nn.sigmoid(_dot(xg, g1_ref[...])).astype(BF16)
    gate_ref[0] = _dot(hg, g2_ref[...])
    xn_ref[0] = xn.astype(BF16)

    @pl.when(i == last_tile)
    def _():
        shift_ref[0] = xn[last_row:last_row + 1, :]


def _prologue(x, sprev, g, mu, w1, a1, g1, w2, a2, g2, w0, a0, t_real):
    B, T, D = x.shape
    RW = w2.shape[1]
    tm = _pick_tile(T, (128, 64, 32, 16, 8))
    nt = T // tm
    full = lambda arr: pl.BlockSpec(arr.shape, lambda b, i: (0,) * arr.ndim)
    row_spec = lambda w: pl.BlockSpec((1, tm, w), lambda b, i: (b, i, 0))
    kern = functools.partial(_prologue_kernel, last_tile=(t_real - 1) // tm,
                             last_row=(t_real - 1) % tm)
    return pl.pallas_call(
        kern,
        grid=(B, nt),
        in_specs=[row_spec(D), pl.BlockSpec((1, 1, D), lambda b, i: (b, 0, 0)), full(g), full(mu),
                  full(w1), full(a1), full(g1), full(w2), full(a2), full(g2), full(w0), full(a0)],
        out_specs=[row_spec(D), row_spec(RW), row_spec(RW), row_spec(RW),
                   pl.BlockSpec((1, 1, D), lambda b, i: (b, 0, 0))],
        out_shape=[jax.ShapeDtypeStruct((B, T, D), BF16), jax.ShapeDtypeStruct((B, T, RW), F32),
                   jax.ShapeDtypeStruct((B, T, RW), F32), jax.ShapeDtypeStruct((B, T, RW), F32),
                   jax.ShapeDtypeStruct((B, 1, D), F32)],
        scratch_shapes=[pltpu.VMEM((1, D), F32)],
        compiler_params=_cparams("arbitrary", "arbitrary"),
    )(x, sprev, g, mu, w1, a1, g1, w2, a2, g2, w0, a0)


def _rmsnorm_kernel(x_ref, g_ref, o_ref):
    x = x_ref[...]
    y = x * lax.rsqrt(jnp.mean(x * x, axis=-1, keepdims=True) + RMS_EPS) * g_ref[...]
    o_ref[...] = y.astype(o_ref.dtype)


def _rmsnorm(x2d, g, out_dtype):
    M, D = x2d.shape
    tm = _pick_tile(M, (256, 128, 64, 32, 16, 8))
    return pl.pallas_call(
        _rmsnorm_kernel,
        grid=(M // tm,),
        in_specs=[pl.BlockSpec((tm, D), lambda i: (i, 0)), pl.BlockSpec((1, D), lambda i: (0, 0))],
        out_specs=pl.BlockSpec((tm, D), lambda i: (i, 0)),
        out_shape=jax.ShapeDtypeStruct((M, D), out_dtype),
        compiler_params=_cparams("arbitrary"),
    )(x2d, g)


def _proj_kernel(x_ref, w_ref, o_ref):
    o_ref[...] = _dot(x_ref[...], w_ref[...])


def _proj_rope_kernel(x_ref, w_ref, cos_ref, sin_ref, o_ref, *, head_dim):
    y = _dot(x_ref[...], w_ref[...])
    cos = cos_ref[...]
    sin = sin_ref[...]
    for h in range(y.shape[1] // head_dim):
        yh = y[:, h * head_dim:(h + 1) * head_dim]
        o_ref[:, h * head_dim:(h + 1) * head_dim] = (
            yh * cos + pltpu.roll(yh, head_dim // 2, axis=1) * sin)


def _project(x2d, w_all, layer, col_off, n_cols, rope=None):
    M, K = x2d.shape
    row_prefs = (1024, 512, 256, 128, 64, 32, 16, 8)
    if rope is not None and M > row_prefs[0]:
        tm = _pick_tile(math.gcd(M, rope[3]), row_prefs)
    else:
        tm = _pick_tile(M, row_prefs)
    tn = _pick_tile(math.gcd(n_cols, col_off) if col_off else n_cols, (512, 256, 128))
    off = col_off // tn
    in_specs = [pl.BlockSpec((tm, K), lambda i, j: (i, 0)),
                pl.BlockSpec((None, K, tn), lambda i, j: (layer, 0, j + off))]
    args = [x2d, w_all]
    kern = _proj_kernel
    if rope is not None:
        cos, sin, head_dim, T = rope
        if tm <= T:
            per = T // tm
            tab_map = lambda i, j: (i % per, 0)
        else:
            cos = jnp.tile(cos, (tm // T, 1))
            sin = jnp.tile(sin, (tm // T, 1))
            tab_map = lambda i, j: (0, 0)
        in_specs += [pl.BlockSpec((tm, head_dim), tab_map), pl.BlockSpec((tm, head_dim), tab_map)]
        args += [cos, sin]
        kern = functools.partial(_proj_rope_kernel, head_dim=head_dim)
    return pl.pallas_call(
        kern,
        grid=(M // tm, n_cols // tn),
        in_specs=in_specs,
        out_specs=pl.BlockSpec((tm, tn), lambda i, j: (i, j)),
        out_shape=jax.ShapeDtypeStruct((M, n_cols), F32),
        compiler_params=_cparams("arbitrary", "arbitrary"),
    )(*args)


def _mm_res_kernel(*refs, n_lhs):
    res_ref = refs[2 * n_lhs]
    o_ref = refs[2 * n_lhs + 1]
    acc = res_ref[...]
    for t in range(n_lhs):
        acc = acc + _dot(refs[t][...], refs[n_lhs + t][...])
    o_ref[...] = acc


def _matmul_residual(lhs_list, w_all, layer, res2d):
    M, N = res2d.shape
    n_lhs = len(lhs_list)
    ks = [l.shape[1] for l in lhs_list]
    kblk = ks[0]
    assert all(k == kblk for k in ks)
    for tm_pref, tn_pref in ((1024, 512), (512, 512), (512, 256), (256, 256), (256, 128)):
        tm = _pick_tile(M, tuple(p for p in (1024, 512, 256, 128, 64, 32, 16, 8) if p <= tm_pref))
        tn = _pick_tile(N, tuple(p for p in (512, 256, 128) if p <= tn_pref))
        if 2 * (n_lhs * kblk * (tm + tn) * 2 + 2 * tm * tn * 4) <= VMEM_LIMIT * 3 // 4:
            break
    in_specs = [pl.BlockSpec((tm, kblk), lambda i, j: (i, 0)) for _ in lhs_list]
    in_specs += [pl.BlockSpec((None, kblk, tn), functools.partial(lambda i, j, t: (layer, t, j), t=t))
                 for t in range(n_lhs)]
    in_specs += [pl.BlockSpec((tm, tn), lambda i, j: (i, j))]
    return pl.pallas_call(
        functools.partial(_mm_res_kernel, n_lhs=n_lhs),
        grid=(M // tm, N // tn),
        in_specs=in_specs,
        out_specs=pl.BlockSpec((tm, tn), lambda i, j: (i, j)),
        out_shape=jax.ShapeDtypeStruct((M, N), F32),
        compiler_params=_cparams("arbitrary", "arbitrary"),
    )(*lhs_list, *([w_all] * n_lhs), res2d)


def _swiglu_kernel(x_ref, wg_ref, wu_ref, o_ref):
    x = x_ref[...]
    g = _dot(x, wg_ref[...])
    u = _dot(x, wu_ref[...])
    o_ref[...] = (g * jax.nn.sigmoid(g) * u).astype(o_ref.dtype)


def _swiglu(x2d, wg_all, wu_all, layer):
    M, K = x2d.shape
    N = wg_all.shape[2]
    tm = _pick_tile(M, (1024, 512, 256, 128, 64, 32, 16, 8))
    tn = _pick_tile(N, (256, 128))
    wspec = pl.BlockSpec((None, K, tn), lambda i, j: (layer, 0, j))
    return pl.pallas_call(
        _swiglu_kernel,
        grid=(M // tm, N // tn),
        in_specs=[pl.BlockSpec((tm, K), lambda i, j: (i, 0)), wspec, wspec],
        out_specs=pl.BlockSpec((tm, tn), lambda i, j: (i, j)),
        out_shape=jax.ShapeDtypeStruct((M, N), BF16),
        compiler_params=_cparams("arbitrary", "arbitrary"),
    )(x2d, wg_all, wu_all)


def _rank_select(g, n_rows, row_of):
    row = lax.broadcasted_iota(jnp.int32, g.shape, 0)

    def body(m, cnt):
        gm = row_of(m)
        better = (gm > g) | ((gm == g) & (m < row))
        return cnt + better.astype(jnp.int32)

    cnt = lax.fori_loop(0, n_rows, body, jnp.zeros(g.shape, jnp.int32))
    return cnt < MOBA_TOPK


def _moba_prompt_kernel(q_ref, k_ref, v_ref, o_ref, means_ref, kb_ref, vt_ref, gate_ref, sel_ref,
                        *, nb, scale, n_heads, head_dim):
    qi = pl.program_id(2)
    blk = MOBA_BLOCK
    heads = range(n_heads)
    cols = [slice(h * head_dim, (h + 1) * head_dim) for h in heads]

    @pl.when(qi == 0)
    def _():
        means_ref[...] = jnp.zeros(means_ref.shape, F32)
        for h in heads:
            for n in range(nb):
                kn = k_ref[0, n * blk:(n + 1) * blk, cols[h]]
                means_ref[h, n:n + 1, :] = jnp.mean(kn, axis=0, keepdims=True)
                kb_ref[h, n] = kn.astype(BF16)
                vt_ref[h, n] = v_ref[0, n * blk:(n + 1) * blk, cols[h]].T.astype(BF16)

    q = [q_ref[0, :, cols[h]] for h in heads]
    qb = [z.astype(BF16) for z in q]

    for h in heads:
        mh, ml, _ = _split3(means_ref[h])
        qh, ql, _ = _split3(q[h])
        gate = _dot_nt(mh, qh) + _dot_nt(mh, ql) + _dot_nt(ml, qh)
        row = lax.broadcasted_iota(jnp.int32, gate.shape, 0)
        gate_ref[h] = jnp.where(row < qi, gate, NEG_INF)
    for h in heads:
        g = gate_ref[h]
        row = lax.broadcasted_iota(jnp.int32, g.shape, 0)
        sel = (row < qi) & _rank_select(g, nb, lambda m, h=h: gate_ref[h, pl.ds(m, 1), :])
        sel_f = sel.astype(F32)
        for n in range(nb):
            sel_ref[h, n] = sel_f[n:n + 1, :]

    s = [_dot_nt(kb_ref[h, qi], qb[h]) * scale for h in heads]
    kidx = lax.broadcasted_iota(jnp.int32, s[0].shape, 0)
    qidx = lax.broadcasted_iota(jnp.int32, s[0].shape, 1)
    s = [jnp.where(kidx <= qidx, z, NEG_INF) for z in s]
    m0 = [jnp.max(z, axis=0, keepdims=True) for z in s]
    p = [jnp.exp(s[h] - m0[h]) for h in heads]
    l0 = [jnp.sum(z, axis=0, keepdims=True) for z in p]
    acc0 = [_dot(vt_ref[h, qi], p[h].astype(BF16)) for h in heads]

    def body(n, carry):
        m, l, acc = carry
        s = [_dot_nt(kb_ref[h, n], qb[h]) * scale for h in heads]
        s = [jnp.where(sel_ref[h, n] > 0.0, s[h], NEG_INF) for h in heads]
        m_new = [jnp.maximum(m[h], jnp.max(s[h], axis=0, keepdims=True)) for h in heads]
        alpha = [jnp.exp(m[h] - m_new[h]) for h in heads]
        p = [jnp.exp(s[h] - m_new[h]) for h in heads]
        l = [l[h] * alpha[h] + jnp.sum(p[h], axis=0, keepdims=True) for h in heads]
        pv = [_dot(vt_ref[h, n], p[h].astype(BF16)) for h in heads]
        acc = [acc[h] * alpha[h] + pv[h] for h in heads]
        return m_new, l, acc

    _, l, acc = lax.fori_loop(0, qi, body, (m0, l0, acc0))
    for h in heads:
        o_ref[0, :, cols[h]] = (acc[h] / l[h]).T.astype(o_ref.dtype)


MOBA_HEADS_PER_STEP = 4


def _moba_prompt(q, k, v, head_dim):
    B, T, W = q.shape
    assert T % MOBA_BLOCK == 0 and head_dim == LANES
    H = W // head_dim
    hp = _pick_tile(H, (MOBA_HEADS_PER_STEP, 1))
    wp = hp * head_dim
    nb = T // MOBA_BLOCK
    nbp = _round_up(nb, SUBLANES)
    kern = functools.partial(_moba_prompt_kernel, nb=nb, scale=head_dim ** -0.5, n_heads=hp,
                             head_dim=head_dim)
    kv_spec = pl.BlockSpec((1, T, wp), lambda b, h, i: (b, 0, h))
    return pl.pallas_call(
        kern,
        grid=(B, H // hp, nb),
        in_specs=[pl.BlockSpec((1, MOBA_BLOCK, wp), lambda b, h, i: (b, i, h)), kv_spec, kv_spec],
        out_specs=pl.BlockSpec((1, MOBA_BLOCK, wp), lambda b, h, i: (b, i, h)),
        out_shape=jax.ShapeDtypeStruct((B, T, W), BF16),
        scratch_shapes=[pltpu.VMEM((hp, nbp, head_dim), F32),
                        pltpu.VMEM((hp, nb, MOBA_BLOCK, head_dim), BF16),
                        pltpu.VMEM((hp, nb, head_dim, MOBA_BLOCK), BF16),
                        pltpu.VMEM((hp, nbp, MOBA_BLOCK), F32),
                        pltpu.VMEM((hp, nbp, 1, MOBA_BLOCK), F32)],
        compiler_params=_cparams("arbitrary", "arbitrary", "arbitrary"),
    )(q, k, v)


def _moba_route_kernel(pt_ref, q_ref, *rest, ppb, nbk, n_heads):
    k_refs = rest[:ppb]
    ids_ref, gate_ref = rest[ppb:ppb + 2]
    n = pl.program_id(1)
    dh = k_refs[0].shape[1]
    tot = jnp.zeros((n_heads, dh), F32)
    for r in k_refs:
        tot = tot + jnp.sum(r[...].reshape(-1, n_heads, dh), axis=0)
    mean = tot * (1.0 / MOBA_BLOCK)
    gate_ref[n] = jnp.sum(q_ref[0] * mean[None], axis=-1)

    @pl.when(n == nbk - 1)
    def _():
        g = gate_ref[...]
        blk = lax.broadcasted_iota(jnp.int32, g.shape, 0)

        def body(m, cnt):
            gm = gate_ref[m][None]
            better = (gm > g) | ((gm == g) & (m < blk))
            return cnt + better.astype(jnp.int32)

        rank = lax.fori_loop(0, nbk, body, jnp.zeros(g.shape, jnp.int32))
        for j in range(MOBA_TOPK):
            ids_ref[0, j] = jnp.sum(jnp.where(rank == j, blk, 0), axis=0)


def _moba_gather_kernel(pt_ref, ids_ref, q_ref, ko_ref, vo_ref, ck_ref, cv_ref, o_ref,
                        kbuf, vbuf, sem, *, layer, ppb, n_q, n_heads, scale):
    b = pl.program_id(0)
    n_slots = n_q * MOBA_TOPK
    page = kbuf.shape[3]
    ids_q = ids_ref.shape[1] // (MOBA_TOPK * n_heads)

    def copies(h, slot):
        out = []
        for q in range(n_q):
            for j in range(MOBA_TOPK):
                blk = ids_ref[b, (j * ids_q + q) * n_heads + h]
                for p in range(ppb):
                    pid = pt_ref[b, blk * ppb + p]
                    s = q * MOBA_TOPK + j
                    out.append(pltpu.make_async_copy(ck_ref.at[layer, pid, :, h, :], kbuf.at[slot, s, p],
                                                     sem.at[0, slot]))
                    out.append(pltpu.make_async_copy(cv_ref.at[layer, pid, :, h, :], vbuf.at[slot, s, p],
                                                     sem.at[1, slot]))
        return out

    for c in copies(0, 0):
        c.start()

    def head_body(h, carry):
        slot = lax.rem(h, 2)

        @pl.when(h + 1 < n_heads)
        def _():
            for c in copies(h + 1, 1 - slot):
                c.start()

        qb = q_ref[0, h].astype(BF16)
        s_own = _dot_nt(qb, ko_ref[0, h].astype(BF16)) * scale
        kidx = lax.broadcasted_iota(jnp.int32, s_own.shape, 1)
        qidx = lax.broadcasted_iota(jnp.int32, s_own.shape, 0)
        s_own = jnp.where(kidx <= qidx, s_own, NEG_INF)

        for c in copies(h, slot):
            c.wait()
        kb = kbuf[slot].reshape(n_slots * ppb * page, -1).astype(BF16)
        vb = vbuf[slot].reshape(n_slots * ppb * page, -1).astype(BF16)
        s = _dot_nt(qb, kb) * scale
        col_q = lax.broadcasted_iota(jnp.int32, s.shape, 1) // (MOBA_TOPK * ppb * page)
        row_q = lax.broadcasted_iota(jnp.int32, s.shape, 0)
        s = jnp.where(col_q == row_q, s, NEG_INF)
        m = jnp.maximum(jnp.max(s_own, axis=1, keepdims=True), jnp.max(s, axis=1, keepdims=True))
        p_own = jnp.exp(s_own - m)
        p = jnp.exp(s - m)
        l = jnp.sum(p_own, axis=1, keepdims=True) + jnp.sum(p, axis=1, keepdims=True)
        acc = _dot(p_own.astype(BF16), vo_ref[0, h].astype(BF16)) + _dot(p.astype(BF16), vb)
        o_ref[0, h] = (acc / l).astype(o_ref.dtype)
        return carry

    lax.fori_loop(0, n_heads, head_body, 0)


def _moba_sample_gather(q, k_new, v_new, cache_k, cache_v, page_table, layer, n_q):
    B, Ts, W = q.shape
    page, H, dh = cache_k.shape[2:]
    n_pages = page_table.shape[1]
    assert MOBA_BLOCK % page == 0 and (n_pages * page) % MOBA_BLOCK == 0 and dh == LANES
    ppb = MOBA_BLOCK // page
    nbk = n_pages // ppb
    assert nbk >= MOBA_TOPK
    qr = _round_up(n_q, SUBLANES)
    qp = _round_up(n_q, 2 * SUBLANES)
    own = LANES
    assert n_q <= Ts <= own
    pad_rows = lambda z, r: jnp.pad(z, ((0, 0), (0, r - z.shape[1]), (0, 0)))
    heads_first = lambda z: z.reshape(B, -1, H, dh).transpose(0, 2, 1, 3)
    q_route = pad_rows(q[:, :n_q], qr).reshape(B, qr, H, dh)
    q_h = heads_first(pad_rows(q[:, :n_q], qp))
    ko_h, vo_h = heads_first(pad_rows(k_new, own)), heads_first(pad_rows(v_new, own))
    cache_k_flat = cache_k.reshape(cache_k.shape[:2] + (page * H, dh))

    def page_spec(j):
        return pl.BlockSpec((None, None, page * H, dh),
                            lambda b, n, pt: (layer, pt[b, n * ppb + j], 0, 0))

    ids = pl.pallas_call(
        functools.partial(_moba_route_kernel, ppb=ppb, nbk=nbk, n_heads=H),
        grid_spec=pltpu.PrefetchScalarGridSpec(
            num_scalar_prefetch=1,
            grid=(B, nbk),
            in_specs=[pl.BlockSpec((1, qr, H, dh), lambda b, n, pt: (b, 0, 0, 0))]
                     + [page_spec(j) for j in range(ppb)],
            out_specs=pl.BlockSpec((1, MOBA_TOPK, qr, H), lambda b, n, pt: (b, 0, 0, 0)),
            scratch_shapes=[pltpu.VMEM((nbk, qr, H), F32)]),
        out_shape=jax.ShapeDtypeStruct((B, MOBA_TOPK, qr, H), jnp.int32),
        compiler_params=_cparams("arbitrary", "arbitrary"),
    )(page_table, q_route, *([cache_k_flat] * ppb))

    n_slots = n_q * MOBA_TOPK
    per_head = lambda rows: pl.BlockSpec((1, H, rows, dh), lambda b, pt, ids: (b, 0, 0, 0))
    o_h = pl.pallas_call(
        functools.partial(_moba_gather_kernel, layer=layer, ppb=ppb, n_q=n_q, n_heads=H, scale=dh ** -0.5),
        grid_spec=pltpu.PrefetchScalarGridSpec(
            num_scalar_prefetch=2,
            grid=(B,),
            in_specs=[per_head(qp), per_head(own), per_head(own),
                      pl.BlockSpec(memory_space=pl.ANY), pl.BlockSpec(memory_space=pl.ANY)],
            out_specs=per_head(qp),
            scratch_shapes=[pltpu.VMEM((2, n_slots, ppb, page, dh), F32),
                            pltpu.VMEM((2, n_slots, ppb, page, dh), F32),
                            pltpu.SemaphoreType.DMA((2, 2))]),
        out_shape=jax.ShapeDtypeStruct((B, H, qp, dh), BF16),
        compiler_params=_cparams("arbitrary"),
    )(page_table, ids.reshape(B, -1), q_h, ko_h, vo_h, cache_k, cache_v)
    o = o_h.transpose(0, 2, 1, 3).reshape(B, qp, W)
    return pad_rows(o[:, :n_q], Ts)


def _wkv_kernel(*refs, has_state, n_pairs, t_real, n_chunks, head):
    it = iter(refs)
    yr_ref, yk_ref, yv_ref = next(it), next(it), next(it)
    pr_ref, pk_ref, pv_ref = next(it), next(it), next(it)
    lw_ref, a_ref, g_ref = next(it), next(it), next(it)
    mu_ref, pvec_ref = next(it), next(it)
    tri_ref, seg_ref = next(it), next(it)
    s0_ref = next(it) if has_state else None
    o_ref, sout_ref = next(it), next(it)
    sv_ref, carry_ref = next(it), next(it)

    c = pl.program_id(2)
    C = yr_ref.shape[1]
    G = 2 * C
    lane_g = lax.broadcasted_iota(jnp.int32, (G, LANES), 1)
    row_g = lax.broadcasted_iota(jnp.int32, (G, LANES), 0)
    stack_mask = (row_g // C) == (lane_g // head)
    rr = lax.broadcasted_iota(jnp.int32, (G, G), 0)
    cc = lax.broadcasted_iota(jnp.int32, (G, G), 1)
    strict = ((rr // C) == (cc // C)) & ((cc % C) < (rr % C))
    incl_row = (lax.broadcasted_iota(jnp.int32, (C, 2 * G), 1) % C
                <= lax.broadcasted_iota(jnp.int32, (C, 2 * G), 0))
    pr_i = lax.broadcasted_iota(jnp.int32, (head, LANES), 0)
    pc_i = lax.broadcasted_iota(jnp.int32, (head, LANES), 1)
    place = [(pc_i == pr_i + h * head).astype(BF16) for h in range(2)]
    seg = seg_ref[...]
    tri = tri_ref[...]
    trow = lax.broadcasted_iota(jnp.int32, (C, LANES), 0)

    def stack(x):
        return jnp.where(stack_mask, jnp.concatenate([x, x], axis=0), 0.0)

    @pl.when(c == 0)
    def _():
        carry_ref[0:1, :] = pr_ref[0]
        carry_ref[1:2, :] = pk_ref[0]
        carry_ref[2:3, :] = pv_ref[0]
        for p in range(n_pairs):
            if has_state:
                blocks = [_dot_exact_rhs(s0_ref[0, 2 * p + h], place[h]) for h in range(2)]
                sv_ref[p] = jnp.concatenate(blocks, axis=0)
            else:
                sv_ref[p] = jnp.zeros((LANES, LANES), F32)

    pairs = range(n_pairs)
    lanes_of = [slice(p * LANES, (p + 1) * LANES) for p in pairs]

    def mixed(y_ref, idx, sl):
        y = y_ref[0, :, sl]
        prev = jnp.where(trow == 0, carry_ref[idx:idx + 1, sl], pltpu.roll(y, 1, axis=0))
        carry_ref[idx:idx + 1, sl] = y[C - 1:C, :]
        return y + (prev - y) * mu_ref[idx:idx + 1, sl]

    r = [mixed(yr_ref, 0, sl) for sl in lanes_of]
    k = [mixed(yk_ref, 1, sl) for sl in lanes_of]
    v = [mixed(yv_ref, 2, sl) for sl in lanes_of]
    a = [a_ref[0, :, sl] for sl in lanes_of]
    lw = [lw_ref[0, :, sl] for sl in lanes_of]
    kk = [k[p] * pvec_ref[0:1, lanes_of[p]] for p in pairs]
    k2 = [k[p] * (1.0 + (a[p] - 1.0) * pvec_ref[1:2, lanes_of[p]]) for p in pairs]
    sums = [_dot_exact_rhs(jnp.concatenate(
        [kk[p] * kk[p], r[p] * k2[p] * pvec_ref[2:3, lanes_of[p]]], axis=0), seg, WKV_SUM_TERMS)
        for p in pairs]
    kk = [kk[p] / jnp.maximum(jnp.sqrt(sums[p][0:C]), KK_EPS) for p in pairs]
    bonus = [sums[p][C:2 * C] for p in pairs]
    if t_real is not None:
        live = (c * C + trow) < t_real
        lw = [jnp.where(live, z, 0.0) for z in lw]
        kk = [jnp.where(live, z, 0.0) for z in kk]
        k2 = [jnp.where(live, z, 0.0) for z in k2]
        v = [jnp.where(live, z, 0.0) for z in v]
    bv = [kk[p] * a[p] for p in pairs]

    logp = [_dot_exact_lhs(tri, lw[p], WKV_SUM_TERMS) for p in pairs]
    logpc = [z[C - 1:C, :] for z in logp]
    lhs, rhs, vst, bk = [], [], [], []
    for p in pairs:
        inv_p = jnp.exp(-logp[p])
        tail = jnp.exp(logpc[p] - logp[p])
        a_t = -kk[p] * jnp.exp(logp[p] - lw[p])
        r_t = r[p] * jnp.exp(logp[p])
        lhs.append(jnp.concatenate([a_t, r_t], axis=0).astype(BF16))
        rhs.append(jnp.concatenate([stack(bv[p] * inv_p), stack(k2[p] * inv_p)], axis=0).astype(BF16))
        bk.append(jnp.concatenate([stack(bv[p] * tail), stack(k2[p] * tail)], axis=0).astype(BF16))
        vst.append(stack(v[p]).astype(BF16))

    quad = [_dot_nt(lhs[p], rhs[p]) for p in pairs]
    sv = [sv_ref[p] for p in pairs]
    ss = [_dot_nt(lhs[p], sv[p].astype(BF16)) for p in pairs]
    twice = lambda z: jnp.concatenate([z, z], axis=0)
    n_pow = [jnp.where(strict, twice(quad[p][0:C, 0:G]), 0.0).astype(BF16) for p in pairs]
    a_ak = [jnp.where(strict, twice(quad[p][0:C, G:2 * G]), 0.0).astype(BF16) for p in pairs]
    a_r = [jnp.where(incl_row, quad[p][C:2 * C, :], 0.0).astype(BF16) for p in pairs]

    u = [stack(ss[p][0:C]) + _dot(a_ak[p], vst[p]) for p in pairs]
    span = 1
    while span < C:
        u = [u[p] + _dot(n_pow[p], u[p].astype(BF16)) for p in pairs]
        span *= 2
        if span < C:
            n_pow = [_dot(n_pow[p], n_pow[p]).astype(BF16) for p in pairs]
    uv = [jnp.concatenate([u[p].astype(BF16), vst[p]], axis=0) for p in pairs]
    o = [ss[p][C:2 * C] + _dot(a_r[p], uv[p]) for p in pairs]
    for p in pairs:
        sv_ref[p] = sv[p] * jnp.exp(logpc[p]) + _dot_tn(uv[p], bk[p])

    d = [o[p] - _dot_exact_rhs(o[p], seg, WKV_SUM_TERMS) * (1.0 / head) for p in pairs]
    var = [_dot_exact_rhs(d[p] * d[p], seg, WKV_SUM_TERMS) * (1.0 / head) for p in pairs]
    for p in pairs:
        sl = lanes_of[p]
        y = d[p] * lax.rsqrt(var[p] + GN_EPS) * pvec_ref[3:4, sl] + pvec_ref[4:5, sl]
        y = y + bonus[p] * v[p]
        o_ref[0, :, sl] = (y * g_ref[0, :, sl]).astype(o_ref.dtype)

    @pl.when(c == n_chunks - 1)
    def _():
        for p in range(n_pairs):
            for h in range(2):
                rows = sv_ref[p, h * head:(h + 1) * head, :]
                sout_ref[0, 2 * p + h] = _dot_nt(*_pair3(rows, place[h]))


def _pair3(x, e):
    h1, h2, h3 = _split3(x)
    return jnp.concatenate([h1, h2, h3], axis=1), jnp.concatenate([e, e, e], axis=1)


def _wkv(y_rkv, y_prev, lw, a, g, mu_rkv, pvec, state0, t_real):
    B, t_in, RW3 = y_rkv.shape
    RW = RW3 // 3
    head = LANES // 2
    NH = RW // head
    C = WKV_CHUNK
    T = _round_up(t_in, C)
    if T != t_in:
        pad_t = lambda z: jnp.pad(z, ((0, 0), (0, T - t_in), (0, 0)))
        y_rkv, lw, a, g = pad_t(y_rkv), pad_t(lw), pad_t(a), pad_t(g)
    n_chunks = T // C
    n_pairs = _pick_tile(RW // LANES, (8, 4, 2, 1))
    lwd = n_pairs * LANES
    ncol = RW // lwd
    G = 2 * C
    tri = (jnp.arange(C)[:, None] >= jnp.arange(C)[None, :]).astype(BF16)
    seg = ((jnp.arange(LANES)[:, None] // head) == (jnp.arange(LANES)[None, :] // head)).astype(BF16)

    def cols(part):
        return pl.BlockSpec((1, C, lwd), lambda b, j, c: (b, c, j + part * ncol))

    def prev_cols(part):
        return pl.BlockSpec((1, 1, lwd), lambda b, j, c: (b, 0, j + part * ncol))

    tile = pl.BlockSpec((1, C, lwd), lambda b, j, c: (b, c, j))
    in_specs = [cols(0), cols(1), cols(2), prev_cols(0), prev_cols(1), prev_cols(2), tile, tile, tile,
                pl.BlockSpec((3, lwd), lambda b, j, c: (0, j)),
                pl.BlockSpec((SUBLANES, lwd), lambda b, j, c: (0, j)),
                pl.BlockSpec((C, C), lambda b, j, c: (0, 0)),
                pl.BlockSpec((LANES, LANES), lambda b, j, c: (0, 0))]
    args = [y_rkv, y_rkv, y_rkv, y_prev, y_prev, y_prev, lw, a, g, mu_rkv, pvec, tri, seg]
    state_spec = pl.BlockSpec((1, 2 * n_pairs, head, head), lambda b, j, c: (b, j, 0, 0))
    if state0 is not None:
        in_specs.append(state_spec)
        args.append(state0)
    kern = functools.partial(_wkv_kernel, has_state=state0 is not None, n_pairs=n_pairs,
                             t_real=None if t_real == T else t_real, n_chunks=n_chunks, head=head)
    o, state = pl.pallas_call(
        kern,
        grid=(B, ncol, n_chunks),
        in_specs=in_specs,
        out_specs=[tile, state_spec],
        out_shape=[jax.ShapeDtypeStruct((B, T, RW), BF16),
                   jax.ShapeDtypeStruct((B, NH, head, head), F32)],
        scratch_shapes=[pltpu.VMEM((n_pairs, LANES, LANES), F32), pltpu.VMEM((SUBLANES, lwd), F32)],
        compiler_params=_cparams("arbitrary", "arbitrary", "arbitrary"),
    )(*args)
    return o[:, :t_in], state


def _rope_tables(pos, head_dim):
    half = head_dim // 2
    inv_freq = ROPE_THETA ** (-jnp.arange(half, dtype=F32) / half)
    ang = pos.astype(F32)[:, None] * inv_freq[None, :]
    cos, sin = jnp.cos(ang), jnp.sin(ang)
    return jnp.concatenate([cos, cos], axis=1), jnp.concatenate([-sin, sin], axis=1)


def _pad_cols(w, mult):
    pad = _round_up(w.shape[-1], mult) - w.shape[-1]
    return jnp.pad(w, [(0, 0)] * (w.ndim - 1) + [(0, pad)]) if pad else w


def _pad_rows(w, mult):
    pad = _round_up(w.shape[-2], mult) - w.shape[-2]
    return jnp.pad(w, [(0, 0)] * (w.ndim - 2) + [(0, pad), (0, 0)]) if pad else w


def kernel(x_prompt, x_sample, cache_k, cache_v, state_wkv, state_shift, page_table, norm_mix, norm_ffn, norm_final, w_in, w_out, mu_rkv, mu_wag, decay_w0, decay_w1, decay_w2, aaa_a0, aaa_a1, aaa_a2, gate_g1, gate_g2, k_k, k_a, r_k, ln_x_w, ln_x_b, ffn_w_gate, ffn_w_up, ffn_w_down):
    depth = w_in.shape[0]
    D = x_prompt.shape[-1]
    n_att_heads, head_dim = cache_k.shape[3], cache_k.shape[4]
    att_w = n_att_heads * head_dim
    rw = mu_rkv.shape[-1]
    n_rwkv_heads, rwkv_head = r_k.shape[1], r_k.shape[2]
    assert rwkv_head * 2 == LANES and head_dim == LANES
    Bp, Tp, _ = x_prompt.shape
    Bs, Ts, _ = x_sample.shape
    Tsp = _round_up(Ts, SUBLANES)
    past_len = page_table.shape[1] * cache_k.shape[2]

    w_in_b, w_out_b = w_in.astype(BF16), w_out.astype(BF16)
    wg_b, wu_b, wd_b = ffn_w_gate.astype(BF16), ffn_w_up.astype(BF16), ffn_w_down.astype(BF16)
    w1_b, a1_b = _pad_cols(decay_w1, LANES).astype(BF16), _pad_cols(aaa_a1, LANES).astype(BF16)
    g1_b = _pad_cols(gate_g1, LANES).astype(BF16)
    w2_b, a2_b = _pad_rows(decay_w2, LANES).astype(BF16), _pad_rows(aaa_a2, LANES).astype(BF16)
    g2_b = _pad_rows(gate_g2, LANES).astype(BF16)

    cos_p, sin_p = _rope_tables(jnp.arange(Tp, dtype=jnp.int32), head_dim)
    cos_s, sin_s = _rope_tables(past_len + jnp.arange(Tsp, dtype=jnp.int32), head_dim)

    x_s = jnp.pad(x_sample, ((0, 0), (0, Tsp - Ts), (0, 0)))
    groups = [
        dict(x=x_prompt, B=Bp, T=Tp, t_real=Tp, cos=cos_p, sin=sin_p, prompt=True),
        dict(x=x_s, B=Bs, T=Tsp, t_real=Ts, cos=cos_s, sin=sin_s, prompt=False),
    ]
    outs = [dict(k=[], v=[], wkv=[], shift=[]) for _ in groups]

    for layer in range(depth):
        pvec = jnp.stack([k_k[layer], k_a[layer], r_k[layer].reshape(-1), ln_x_w[layer], ln_x_b[layer]]
                         + [jnp.zeros((rw,), F32)] * (SUBLANES - 5))
        for grp, out in zip(groups, outs):
            B, T, x = grp["B"], grp["T"], grp["x"]
            M = B * T
            if grp["prompt"]:
                sprev = jnp.zeros((B, 1, D), F32)
                state0 = None
            else:
                sprev = state_shift[layer][:, None, :]
                state0 = state_wkv[layer]
            xn, lw, a, g, shift = _prologue(
                x, sprev, norm_mix[layer][None], mu_wag[layer], w1_b[layer], a1_b[layer], g1_b[layer],
                w2_b[layer], a2_b[layer], g2_b[layer], decay_w0[layer][None], aaa_a0[layer][None],
                grp["t_real"])
            xn2 = xn.reshape(M, D)
            rope = (grp["cos"], grp["sin"], head_dim, T)
            q = _project(xn2, w_in_b, layer, 0, att_w, rope).reshape(B, T, att_w)
            k = _project(xn2, w_in_b, layer, att_w, att_w, rope).reshape(B, T, att_w)
            v = _project(xn2, w_in_b, layer, 2 * att_w, att_w).reshape(B, T, att_w)
            y_rkv = _project(xn2, w_in_b, layer, 3 * att_w, 3 * rw).reshape(B, T, 3 * rw)
            if grp["prompt"]:
                y_prev = jnp.zeros((B, 1, 3 * rw), F32)
                o_att = _moba_prompt(q, k, v, head_dim)
            else:
                sp_b = _pad_rows(sprev.reshape(B, D), SUBLANES).astype(BF16)
                y_prev = _project(sp_b, w_in_b, layer, 3 * att_w, 3 * rw)[:B].reshape(B, 1, 3 * rw)
                o_att = _moba_sample_gather(q, k, v, cache_k, cache_v, page_table, layer, grp["t_real"])
            o_rwkv, wkv_new = _wkv(y_rkv, y_prev, lw, a, g, mu_rkv[layer], pvec, state0, grp["t_real"])
            h = _matmul_residual([o_att.reshape(M, att_w), o_rwkv.reshape(M, rw)], w_out_b, layer,
                                 x.reshape(M, D))
            hn = _rmsnorm(h, norm_ffn[layer][None], BF16)
            act = _swiglu(hn, wg_b, wu_b, layer)
            x_next = _matmul_residual([act], wd_b, layer, h)
            grp["x"] = x_next.reshape(B, T, D)
            tr = grp["t_real"]
            out["k"].append(k[:, :tr].reshape(B, tr, n_att_heads, head_dim))
            out["v"].append(v[:, :tr].reshape(B, tr, n_att_heads, head_dim))
            out["wkv"].append(wkv_new)
            out["shift"].append(shift.reshape(B, D))

    ys = []
    for grp in groups:
        B, T = grp["B"], grp["T"]
        y = _rmsnorm(grp["x"].reshape(B * T, D), norm_final[None], F32).reshape(B, T, D)
        ys.append(y[:, :grp["t_real"]])
    po, so = outs
    return (ys[0], ys[1],
            jnp.stack(po["k"]), jnp.stack(po["v"]), jnp.stack(po["wkv"]), jnp.stack(po["shift"]),
            jnp.stack(so["k"]), jnp.stack(so["v"]), jnp.stack(so["wkv"]), jnp.stack(so["shift"]))
```

```python
import functools
import math

import jax
import jax.numpy as jnp
from jax import lax
from jax.experimental import pallas as pl
from jax.experimental.pallas import tpu as pltpu

MOBA_BLOCK = 256
MOBA_TOPK = 3
ROPE_THETA = 10000.0
RMS_EPS = 1e-6
GN_EPS = 64e-5
KK_EPS = 1e-12
DECAY_SCALE = math.exp(-0.5)

LANES = 128
SUBLANES = 8
WKV_CHUNK = 64
WKV_SUM_TERMS = 2
VMEM_LIMIT = 56 * 1024 * 1024

F32 = jnp.float32
BF16 = jnp.bfloat16
NEG_INF = float("-inf")


def _cparams(*sem):
    return pltpu.CompilerParams(dimension_semantics=sem, vmem_limit_bytes=VMEM_LIMIT)


def _round_up(n, m):
    return -(-n // m) * m


def _pick_tile(n, prefs):
    for p in prefs:
        if n % p == 0:
            return p
    return n


def _split3(x):
    h1 = x.astype(BF16)
    r1 = x - h1.astype(F32)
    h2 = r1.astype(BF16)
    h3 = (r1 - h2.astype(F32)).astype(BF16)
    return h1, h2, h3


def _dot(a, b):
    return jnp.dot(a, b, preferred_element_type=F32)


def _dot_nt(a, b):
    return lax.dot_general(a, b, (((1,), (1,)), ((), ())), preferred_element_type=F32)


def _dot_tn(a, b):
    return lax.dot_general(a, b, (((0,), (0,)), ((), ())), preferred_element_type=F32)


def _dot_exact_rhs(x, e, terms=3):
    m = x.shape[0]
    y = _dot(jnp.concatenate(_split3(x)[:terms], axis=0), e)
    return sum(y[i * m:(i + 1) * m] for i in range(1, terms)) + y[0:m]


def _dot_exact_lhs(e, x, terms=3):
    n = x.shape[1]
    y = _dot(e, jnp.concatenate(_split3(x)[:terms], axis=1))
    return sum(y[:, i * n:(i + 1) * n] for i in range(1, terms)) + y[:, 0:n]


def _prologue_kernel(x_ref, sprev_ref, g_ref, mu_ref, w1_ref, a1_ref, g1_ref, w2_ref, a2_ref,
                     g2_ref, w0_ref, a0_ref, xn_ref, lw_ref, a_ref, gate_ref, shift_ref,
                     carry_ref, *, last_tile, last_row):
    i = pl.program_id(1)
    x = x_ref[0]
    tm = x.shape[0]
    xn = x * lax.rsqrt(jnp.mean(x * x, axis=-1, keepdims=True) + RMS_EPS) * g_ref[...]

    @pl.when(i == 0)
    def _():
        carry_ref[...] = sprev_ref[0]

    row = lax.broadcasted_iota(jnp.int32, xn.shape, 0)
    prev = jnp.where(row == 0, carry_ref[...], pltpu.roll(xn, 1, axis=0))
    carry_ref[...] = xn[tm - 1:tm, :]
    dx = prev - xn
    xw = (xn + dx * mu_ref[0:1, :]).astype(BF16)
    xa = (xn + dx * mu_ref[1:2, :]).astype(BF16)
    xg = (xn + dx * mu_ref[2:3, :]).astype(BF16)

    hw = jnp.tanh(_dot(xw, w1_ref[...])).astype(BF16)
    zw = w0_ref[...] + _dot(hw, w2_ref[...])
    lw_ref[0] = -DECAY_SCALE * jax.nn.sigmoid(zw)
    ha = _dot(xa, a1_ref[...]).astype(BF16)
    a_ref[0] = jax.nn.sigmoid(a0_ref[...] + _dot(ha, a2_ref[...]))
    hg = jax.nn.sigmoid(_dot(xg, g1_ref[...])).astype(BF16)
    gate_ref[0] = _dot(hg, g2_ref[...])
    xn_ref[0] = xn.astype(BF16)

    @pl.when(i == last_tile)
    def _():
        shift_ref[0] = xn[last_row:last_row + 1, :]


def _prologue(x, sprev, g, mu, w1, a1, g1, w2, a2, g2, w0, a0, t_real):
    B, T, D = x.shape
    RW = w2.shape[1]
    tm = _pick_tile(T, (128, 64, 32, 16, 8))
    nt = T // tm
    full = lambda arr: pl.BlockSpec(arr.shape, lambda b, i: (0,) * arr.ndim)
    row_spec = lambda w: pl.BlockSpec((1, tm, w), lambda b, i: (b, i, 0))
    kern = functools.partial(_prologue_kernel, last_tile=(t_real - 1) // tm,
                             last_row=(t_real - 1) % tm)
    return pl.pallas_call(
        kern,
        grid=(B, nt),
        in_specs=[row_spec(D), pl.BlockSpec((1, 1, D), lambda b, i: (b, 0, 0)), full(g), full(mu),
                  full(w1), full(a1), full(g1), full(w2), full(a2), full(g2), full(w0), full(a0)],
        out_specs=[row_spec(D), row_spec(RW), row_spec(RW), row_spec(RW),
                   pl.BlockSpec((1, 1, D), lambda b, i: (b, 0, 0))],
        out_shape=[jax.ShapeDtypeStruct((B, T, D), BF16), jax.ShapeDtypeStruct((B, T, RW), F32),
                   jax.ShapeDtypeStruct((B, T, RW), F32), jax.ShapeDtypeStruct((B, T, RW), F32),
                   jax.ShapeDtypeStruct((B, 1, D), F32)],
        scratch_shapes=[pltpu.VMEM((1, D), F32)],
        compiler_params=_cparams("arbitrary", "arbitrary"),
    )(x, sprev, g, mu, w1, a1, g1, w2, a2, g2, w0, a0)


def _rmsnorm_kernel(x_ref, g_ref, o_ref):
    x = x_ref[...]
    y = x * lax.rsqrt(jnp.mean(x * x, axis=-1, keepdims=True) + RMS_EPS) * g_ref[...]
    o_ref[...] = y.astype(o_ref.dtype)


def _rmsnorm(x2d, g, out_dtype):
    M, D = x2d.shape
    tm = _pick_tile(M, (256, 128, 64, 32, 16, 8))
    return pl.pallas_call(
        _rmsnorm_kernel,
        grid=(M // tm,),
        in_specs=[pl.BlockSpec((tm, D), lambda i: (i, 0)), pl.BlockSpec((1, D), lambda i: (0, 0))],
        out_specs=pl.BlockSpec((tm, D), lambda i: (i, 0)),
        out_shape=jax.ShapeDtypeStruct((M, D), out_dtype),
        compiler_params=_cparams("arbitrary"),
    )(x2d, g)


_WS_EXTRAS = {"plain": 0, "rope": 2, "swiglu": 0, "residual": 1}


def _ws_kernel(*refs, n_lhs, n_grp, mode, head_dim):
    it = iter(refs)
    lhs_p = [next(it) for _ in range(n_lhs)]
    lhs_s = [next(it) for _ in range(n_lhs)]
    w = [[next(it) for _ in range(n_lhs)] for _ in range(n_grp)]
    extra_p = [next(it) for _ in range(_WS_EXTRAS[mode])]
    extra_s = [next(it) for _ in range(_WS_EXTRAS[mode])]
    o_p, o_s = next(it), next(it)
    wb = [[next(it) for _ in range(n_lhs)] for _ in range(n_grp)]

    def emit(lhs, extra, o_ref):
        ys = []
        for g in range(n_grp):
            y = _dot(lhs[0][...], wb[g][0][...])
            for t in range(1, n_lhs):
                y = y + _dot(lhs[t][...], wb[g][t][...])
            ys.append(y)
        if mode == "plain":
            o_ref[...] = ys[0]
        elif mode == "residual":
            o_ref[...] = extra[0][...] + ys[0]
        elif mode == "swiglu":
            o_ref[...] = (ys[0] * jax.nn.sigmoid(ys[0]) * ys[1]).astype(o_ref.dtype)
        else:
            cos, sin = extra[0][...], extra[1][...]
            for h in range(ys[0].shape[1] // head_dim):
                yh = ys[0][:, h * head_dim:(h + 1) * head_dim]
                o_ref[:, h * head_dim:(h + 1) * head_dim] = (
                    yh * cos + pltpu.roll(yh, head_dim // 2, axis=1) * sin)

    @pl.when(pl.program_id(1) == 0)
    def _():
        for g in range(n_grp):
            for t in range(n_lhs):
                wb[g][t][...] = w[g][t][...].astype(BF16)
        emit(lhs_s, extra_s, o_s)

    emit(lhs_p, extra_p, o_p)


def _ws_matmul(lhs_p, lhs_s, weights, layer, col_off, n_cols, mode, out_dtype, extra_p=(), extra_s=(),
               rows_per_seq=None, head_dim=LANES):
    n_lhs, n_grp = len(lhs_p), len(weights)
    M, kblk = lhs_p[0].shape
    Ms = lhs_s[0].shape[0]
    tm = _pick_tile(math.gcd(M, rows_per_seq) if rows_per_seq else M, (1024, 512, 256, 128, 64, 32, 16, 8))
    out_bytes = jnp.dtype(out_dtype).itemsize
    for tn_pref in (512, 256, 128):
        tn = _pick_tile(math.gcd(n_cols, col_off) if col_off else n_cols,
                        tuple(p for p in (512, 256, 128) if p <= tn_pref))
        w_elems = n_grp * n_lhs * kblk * tn
        need = (w_elems * (2 * 4 + 2) + 2 * n_lhs * kblk * (tm + Ms) * 2
                + 2 * (tm + Ms) * tn * (out_bytes + (4 if mode == "residual" else 0)))
        if need <= VMEM_LIMIT * 7 // 8:
            break
    off = col_off // tn
    row_p = lambda width: pl.BlockSpec((tm, width), lambda j, i: (i, 0))
    all_s = lambda width: pl.BlockSpec((Ms, width), lambda j, i: (0, 0))
    in_specs = [row_p(kblk)] * n_lhs + [all_s(kblk)] * n_lhs
    in_specs += [pl.BlockSpec((None, kblk, tn), functools.partial(lambda j, i, t: (layer, t, j + off), t=t))
                 for _ in range(n_grp) for t in range(n_lhs)]
    args = list(lhs_p) + list(lhs_s) + [wt for wt in weights for _ in range(n_lhs)]
    if mode == "rope":
        per = rows_per_seq // tm
        in_specs += [pl.BlockSpec((tm, head_dim), lambda j, i: (i % per, 0))] * 2 + [all_s(head_dim)] * 2
    elif mode == "residual":
        in_specs += [pl.BlockSpec((tm, tn), lambda j, i: (i, j)), pl.BlockSpec((Ms, tn), lambda j, i: (0, j))]
    args += list(extra_p) + list(extra_s)
    return pl.pallas_call(
        functools.partial(_ws_kernel, n_lhs=n_lhs, n_grp=n_grp, mode=mode, head_dim=head_dim),
        grid=(n_cols // tn, M // tm),
        in_specs=in_specs,
        out_specs=[pl.BlockSpec((tm, tn), lambda j, i: (i, j)), pl.BlockSpec((Ms, tn), lambda j, i: (0, j))],
        out_shape=[jax.ShapeDtypeStruct((M, n_cols), out_dtype), jax.ShapeDtypeStruct((Ms, n_cols), out_dtype)],
        scratch_shapes=[pltpu.VMEM((kblk, tn), BF16) for _ in range(n_grp * n_lhs)],
        compiler_params=_cparams("arbitrary", "arbitrary"),
    )(*args)


def _mm_res_kernel(*refs, n_lhs):
    res_ref = refs[2 * n_lhs]
    o_ref = refs[2 * n_lhs + 1]
    acc = res_ref[...]
    for t in range(n_lhs):
        acc = acc + _dot(refs[t][...], refs[n_lhs + t][...])
    o_ref[...] = acc


def _matmul_residual(lhs_list, w_all, layer, res2d):
    M, N = res2d.shape
    n_lhs = len(lhs_list)
    ks = [l.shape[1] for l in lhs_list]
    kblk = ks[0]
    assert all(k == kblk for k in ks)
    for tm_pref, tn_pref in ((1024, 512), (512, 512), (512, 256), (256, 256), (256, 128)):
        tm = _pick_tile(M, tuple(p for p in (1024, 512, 256, 128, 64, 32, 16, 8) if p <= tm_pref))
        tn = _pick_tile(N, tuple(p for p in (512, 256, 128) if p <= tn_pref))
        if 2 * (n_lhs * kblk * (tm + tn) * 2 + 2 * tm * tn * 4) <= VMEM_LIMIT * 3 // 4:
            break
    in_specs = [pl.BlockSpec((tm, kblk), lambda i, j: (i, 0)) for _ in lhs_list]
    in_specs += [pl.BlockSpec((None, kblk, tn), functools.partial(lambda i, j, t: (layer, t, j), t=t))
                 for t in range(n_lhs)]
    in_specs += [pl.BlockSpec((tm, tn), lambda i, j: (i, j))]
    return pl.pallas_call(
        functools.partial(_mm_res_kernel, n_lhs=n_lhs),
        grid=(M // tm, N // tn),
        in_specs=in_specs,
        out_specs=pl.BlockSpec((tm, tn), lambda i, j: (i, j)),
        out_shape=jax.ShapeDtypeStruct((M, N), F32),
        compiler_params=_cparams("arbitrary", "arbitrary"),
    )(*lhs_list, *([w_all] * n_lhs), res2d)


def _rank_select(g, n_rows, row_of):
    row = lax.broadcasted_iota(jnp.int32, g.shape, 0)

    def body(m, cnt):
        gm = row_of(m)
        better = (gm > g) | ((gm == g) & (m < row))
        return cnt + better.astype(jnp.int32)

    cnt = lax.fori_loop(0, n_rows, body, jnp.zeros(g.shape, jnp.int32))
    return cnt < MOBA_TOPK


def _moba_prompt_kernel(q_ref, k_ref, v_ref, o_ref, means_ref, kb_ref, vt_ref, gate_ref, sel_ref,
                        *, nb, scale, n_heads, head_dim):
    qi = pl.program_id(2)
    blk = MOBA_BLOCK
    heads = range(n_heads)
    cols = [slice(h * head_dim, (h + 1) * head_dim) for h in heads]

    @pl.when(qi == 0)
    def _():
        means_ref[...] = jnp.zeros(means_ref.shape, F32)
        for h in heads:
            for n in range(nb):
                kn = k_ref[0, n * blk:(n + 1) * blk, cols[h]]
                means_ref[h, n:n + 1, :] = jnp.mean(kn, axis=0, keepdims=True)
                kb_ref[h, n] = kn.astype(BF16)
                vt_ref[h, n] = v_ref[0, n * blk:(n + 1) * blk, cols[h]].T.astype(BF16)

    q = [q_ref[0, :, cols[h]] for h in heads]
    qb = [z.astype(BF16) for z in q]

    for h in heads:
        mh, ml, _ = _split3(means_ref[h])
        qh, ql, _ = _split3(q[h])
        gate = _dot_nt(mh, qh) + _dot_nt(mh, ql) + _dot_nt(ml, qh)
        row = lax.broadcasted_iota(jnp.int32, gate.shape, 0)
        gate_ref[h] = jnp.where(row < qi, gate, NEG_INF)
    for h in heads:
        g = gate_ref[h]
        row = lax.broadcasted_iota(jnp.int32, g.shape, 0)
        sel = (row < qi) & _rank_select(g, nb, lambda m, h=h: gate_ref[h, pl.ds(m, 1), :])
        sel_f = sel.astype(F32)
        for n in range(nb):
            sel_ref[h, n] = sel_f[n:n + 1, :]

    s = [_dot_nt(kb_ref[h, qi], qb[h]) * scale for h in heads]
    kidx = lax.broadcasted_iota(jnp.int32, s[0].shape, 0)
    qidx = lax.broadcasted_iota(jnp.int32, s[0].shape, 1)
    s = [jnp.where(kidx <= qidx, z, NEG_INF) for z in s]
    m0 = [jnp.max(z, axis=0, keepdims=True) for z in s]
    p = [jnp.exp(s[h] - m0[h]) for h in heads]
    l0 = [jnp.sum(z, axis=0, keepdims=True) for z in p]
    acc0 = [_dot(vt_ref[h, qi], p[h].astype(BF16)) for h in heads]

    def body(n, carry):
        m, l, acc = carry
        s = [_dot_nt(kb_ref[h, n], qb[h]) * scale for h in heads]
        s = [jnp.where(sel_ref[h, n] > 0.0, s[h], NEG_INF) for h in heads]
        m_new = [jnp.maximum(m[h], jnp.max(s[h], axis=0, keepdims=True)) for h in heads]
        alpha = [jnp.exp(m[h] - m_new[h]) for h in heads]
        p = [jnp.exp(s[h] - m_new[h]) for h in heads]
        l = [l[h] * alpha[h] + jnp.sum(p[h], axis=0, keepdims=True) for h in heads]
        pv = [_dot(vt_ref[h, n], p[h].astype(BF16)) for h in heads]
        acc = [acc[h] * alpha[h] + pv[h] for h in heads]
        return m_new, l, acc

    _, l, acc = lax.fori_loop(0, qi, body, (m0, l0, acc0))
    for h in heads:
        o_ref[0, :, cols[h]] = (acc[h] / l[h]).T.astype(o_ref.dtype)


MOBA_HEADS_PER_STEP = 4
MOBA_ROUTE_BLOCKS_PER_STEP = 4


def _moba_prompt(q, k, v, head_dim):
    B, T, W = q.shape
    assert T % MOBA_BLOCK == 0 and head_dim == LANES
    H = W // head_dim
    hp = _pick_tile(H, (MOBA_HEADS_PER_STEP, 1))
    wp = hp * head_dim
    nb = T // MOBA_BLOCK
    nbp = _round_up(nb, SUBLANES)
    kern = functools.partial(_moba_prompt_kernel, nb=nb, scale=head_dim ** -0.5, n_heads=hp,
                             head_dim=head_dim)
    kv_spec = pl.BlockSpec((1, T, wp), lambda b, h, i: (b, 0, h))
    return pl.pallas_call(
        kern,
        grid=(B, H // hp, nb),
        in_specs=[pl.BlockSpec((1, MOBA_BLOCK, wp), lambda b, h, i: (b, i, h)), kv_spec, kv_spec],
        out_specs=pl.BlockSpec((1, MOBA_BLOCK, wp), lambda b, h, i: (b, i, h)),
        out_shape=jax.ShapeDtypeStruct((B, T, W), BF16),
        scratch_shapes=[pltpu.VMEM((hp, nbp, head_dim), F32),
                        pltpu.VMEM((hp, nb, MOBA_BLOCK, head_dim), BF16),
                        pltpu.VMEM((hp, nb, head_dim, MOBA_BLOCK), BF16),
                        pltpu.VMEM((hp, nbp, MOBA_BLOCK), F32),
                        pltpu.VMEM((hp, nbp, 1, MOBA_BLOCK), F32)],
        compiler_params=_cparams("arbitrary", "arbitrary", "arbitrary"),
    )(q, k, v)


def _moba_route_kernel(pt_ref, q_ref, *rest, ppb, nbk, n_heads):
    n_pages = len(rest) - 2
    k_refs = rest[:n_pages]
    ids_ref, gate_ref = rest[n_pages:]
    n = pl.program_id(1)
    dh = k_refs[0].shape[1]
    bps = n_pages // ppb
    for bb in range(bps):
        tot = jnp.zeros((n_heads, dh), F32)
        for r in k_refs[bb * ppb:(bb + 1) * ppb]:
            tot = tot + jnp.sum(r[...].reshape(-1, n_heads, dh), axis=0)
        mean = tot * (1.0 / MOBA_BLOCK)
        gate_ref[n * bps + bb] = jnp.sum(q_ref[0] * mean[None], axis=-1)

    @pl.when(n == nbk // bps - 1)
    def _():
        g = gate_ref[...]
        blk = lax.broadcasted_iota(jnp.int32, g.shape, 0)

        def body(m, cnt):
            gm = gate_ref[m][None]
            better = (gm > g) | ((gm == g) & (m < blk))
            return cnt + better.astype(jnp.int32)

        rank = lax.fori_loop(0, nbk, body, jnp.zeros(g.shape, jnp.int32))
        for j in range(MOBA_TOPK):
            ids_ref[0, j] = jnp.sum(jnp.where(rank == j, blk, 0), axis=0)


def _moba_gather_kernel(pt_ref, ids_ref, q_ref, ko_ref, vo_ref, ck_ref, cv_ref, o_ref,
                        kbuf, vbuf, sem, *, layer, ppb, n_q, n_heads, scale):
    b = pl.program_id(0)
    n_slots = n_q * MOBA_TOPK
    page = kbuf.shape[3]
    ids_q = ids_ref.shape[1] // (MOBA_TOPK * n_heads)

    def copies(h, slot):
        out = []
        for q in range(n_q):
            for j in range(MOBA_TOPK):
                blk = ids_ref[b, (j * ids_q + q) * n_heads + h]
                for p in range(ppb):
                    pid = pt_ref[b, blk * ppb + p]
                    s = q * MOBA_TOPK + j
                    out.append(pltpu.make_async_copy(ck_ref.at[layer, pid, :, h, :], kbuf.at[slot, s, p],
                                                     sem.at[0, slot]))
                    out.append(pltpu.make_async_copy(cv_ref.at[layer, pid, :, h, :], vbuf.at[slot, s, p],
                                                     sem.at[1, slot]))
        return out

    for c in copies(0, 0):
        c.start()

    def head_body(h, carry):
        slot = lax.rem(h, 2)

        @pl.when(h + 1 < n_heads)
        def _():
            for c in copies(h + 1, 1 - slot):
                c.start()

        qb = q_ref[0, h].astype(BF16)
        s_own = _dot_nt(qb, ko_ref[0, h].astype(BF16)) * scale
        kidx = lax.broadcasted_iota(jnp.int32, s_own.shape, 1)
        qidx = lax.broadcasted_iota(jnp.int32, s_own.shape, 0)
        s_own = jnp.where(kidx <= qidx, s_own, NEG_INF)

        for c in copies(h, slot):
            c.wait()
        kb = kbuf[slot].reshape(n_slots * ppb * page, -1).astype(BF16)
        vb = vbuf[slot].reshape(n_slots * ppb * page, -1).astype(BF16)
        s = _dot_nt(qb, kb) * scale
        col_q = lax.broadcasted_iota(jnp.int32, s.shape, 1) // (MOBA_TOPK * ppb * page)
        row_q = lax.broadcasted_iota(jnp.int32, s.shape, 0)
        s = jnp.where(col_q == row_q, s, NEG_INF)
        m = jnp.maximum(jnp.max(s_own, axis=1, keepdims=True), jnp.max(s, axis=1, keepdims=True))
        p_own = jnp.exp(s_own - m)
        p = jnp.exp(s - m)
        l = jnp.sum(p_own, axis=1, keepdims=True) + jnp.sum(p, axis=1, keepdims=True)
        acc = _dot(p_own.astype(BF16), vo_ref[0, h].astype(BF16)) + _dot(p.astype(BF16), vb)
        o_ref[0, h] = (acc / l).astype(o_ref.dtype)
        return carry

    lax.fori_loop(0, n_heads, head_body, 0)


def _moba_sample_gather(q, k_new, v_new, cache_k, cache_v, page_table, layer, n_q):
    B, Ts, W = q.shape
    page, H, dh = cache_k.shape[2:]
    n_pages = page_table.shape[1]
    assert MOBA_BLOCK % page == 0 and (n_pages * page) % MOBA_BLOCK == 0 and dh == LANES
    ppb = MOBA_BLOCK // page
    nbk = n_pages // ppb
    assert nbk >= MOBA_TOPK
    qr = _round_up(n_q, SUBLANES)
    qp = _round_up(n_q, 2 * SUBLANES)
    own = LANES
    assert n_q <= Ts <= own
    pad_rows = lambda z, r: jnp.pad(z, ((0, 0), (0, r - z.shape[1]), (0, 0)))
    heads_first = lambda z: z.reshape(B, -1, H, dh).transpose(0, 2, 1, 3)
    q_route = pad_rows(q[:, :n_q], qr).reshape(B, qr, H, dh)
    q_h = heads_first(pad_rows(q[:, :n_q], qp))
    ko_h, vo_h = heads_first(pad_rows(k_new, own)), heads_first(pad_rows(v_new, own))
    cache_k_flat = cache_k.reshape(cache_k.shape[:2] + (page * H, dh))

    bps = _pick_tile(nbk, (MOBA_ROUTE_BLOCKS_PER_STEP, 2, 1))
    pps = ppb * bps

    def page_spec(j):
        return pl.BlockSpec((None, None, page * H, dh),
                            lambda b, n, pt: (layer, pt[b, n * pps + j], 0, 0))

    ids = pl.pallas_call(
        functools.partial(_moba_route_kernel, ppb=ppb, nbk=nbk, n_heads=H),
        grid_spec=pltpu.PrefetchScalarGridSpec(
            num_scalar_prefetch=1,
            grid=(B, nbk // bps),
            in_specs=[pl.BlockSpec((1, qr, H, dh), lambda b, n, pt: (b, 0, 0, 0))]
                     + [page_spec(j) for j in range(pps)],
            out_specs=pl.BlockSpec((1, MOBA_TOPK, qr, H), lambda b, n, pt: (b, 0, 0, 0)),
            scratch_shapes=[pltpu.VMEM((nbk, qr, H), F32)]),
        out_shape=jax.ShapeDtypeStruct((B, MOBA_TOPK, qr, H), jnp.int32),
        compiler_params=_cparams("arbitrary", "arbitrary"),
    )(page_table, q_route, *([cache_k_flat] * pps))

    n_slots = n_q * MOBA_TOPK
    per_head = lambda rows: pl.BlockSpec((1, H, rows, dh), lambda b, pt, ids: (b, 0, 0, 0))
    o_h = pl.pallas_call(
        functools.partial(_moba_gather_kernel, layer=layer, ppb=ppb, n_q=n_q, n_heads=H, scale=dh ** -0.5),
        grid_spec=pltpu.PrefetchScalarGridSpec(
            num_scalar_prefetch=2,
            grid=(B,),
            in_specs=[per_head(qp), per_head(own), per_head(own),
                      pl.BlockSpec(memory_space=pl.ANY), pl.BlockSpec(memory_space=pl.ANY)],
            out_specs=per_head(qp),
            scratch_shapes=[pltpu.VMEM((2, n_slots, ppb, page, dh), F32),
                            pltpu.VMEM((2, n_slots, ppb, page, dh), F32),
                            pltpu.SemaphoreType.DMA((2, 2))]),
        out_shape=jax.ShapeDtypeStruct((B, H, qp, dh), BF16),
        compiler_params=_cparams("arbitrary"),
    )(page_table, ids.reshape(B, -1), q_h, ko_h, vo_h, cache_k, cache_v)
    o = o_h.transpose(0, 2, 1, 3).reshape(B, qp, W)
    return pad_rows(o[:, :n_q], Ts)


def _wkv_kernel(*refs, has_state, n_pairs, t_real, n_chunks, head):
    it = iter(refs)
    yr_ref, yk_ref, yv_ref = next(it), next(it), next(it)
    pr_ref, pk_ref, pv_ref = next(it), next(it), next(it)
    lw_ref, a_ref, g_ref = next(it), next(it), next(it)
    mu_ref, pvec_ref = next(it), next(it)
    tri_ref, seg_ref = next(it), next(it)
    s0_ref = next(it) if has_state else None
    o_ref, sout_ref = next(it), next(it)
    sv_ref, carry_ref = next(it), next(it)

    c = pl.program_id(2)
    C = yr_ref.shape[1]
    G = 2 * C
    lane_g = lax.broadcasted_iota(jnp.int32, (G, LANES), 1)
    row_g = lax.broadcasted_iota(jnp.int32, (G, LANES), 0)
    stack_mask = (row_g // C) == (lane_g // head)
    rr = lax.broadcasted_iota(jnp.int32, (G, G), 0)
    cc = lax.broadcasted_iota(jnp.int32, (G, G), 1)
    strict = ((rr // C) == (cc // C)) & ((cc % C) < (rr % C))
    incl_row = (lax.broadcasted_iota(jnp.int32, (C, 2 * G), 1) % C
                <= lax.broadcasted_iota(jnp.int32, (C, 2 * G), 0))
    pr_i = lax.broadcasted_iota(jnp.int32, (head, LANES), 0)
    pc_i = lax.broadcasted_iota(jnp.int32, (head, LANES), 1)
    place = [(pc_i == pr_i + h * head).astype(BF16) for h in range(2)]
    seg = seg_ref[...]
    tri = tri_ref[...]
    trow = lax.broadcasted_iota(jnp.int32, (C, LANES), 0)

    def stack(x):
        return jnp.where(stack_mask, jnp.concatenate([x, x], axis=0), 0.0)

    @pl.when(c == 0)
    def _():
        carry_ref[0:1, :] = pr_ref[0]
        carry_ref[1:2, :] = pk_ref[0]
        carry_ref[2:3, :] = pv_ref[0]
        for p in range(n_pairs):
            if has_state:
                blocks = [_dot_exact_rhs(s0_ref[0, 2 * p + h], place[h]) for h in range(2)]
                sv_ref[p] = jnp.concatenate(blocks, axis=0)
            else:
                sv_ref[p] = jnp.zeros((LANES, LANES), F32)

    pairs = range(n_pairs)
    lanes_of = [slice(p * LANES, (p + 1) * LANES) for p in pairs]

    def mixed(y_ref, idx, sl):
        y = y_ref[0, :, sl]
        prev = jnp.where(trow == 0, carry_ref[idx:idx + 1, sl], pltpu.roll(y, 1, axis=0))
        carry_ref[idx:idx + 1, sl] = y[C - 1:C, :]
        return y + (prev - y) * mu_ref[idx:idx + 1, sl]

    r = [mixed(yr_ref, 0, sl) for sl in lanes_of]
    k = [mixed(yk_ref, 1, sl) for sl in lanes_of]
    v = [mixed(yv_ref, 2, sl) for sl in lanes_of]
    a = [a_ref[0, :, sl] for sl in lanes_of]
    lw = [lw_ref[0, :, sl] for sl in lanes_of]
    kk = [k[p] * pvec_ref[0:1, lanes_of[p]] for p in pairs]
    k2 = [k[p] * (1.0 + (a[p] - 1.0) * pvec_ref[1:2, lanes_of[p]]) for p in pairs]
    sums = [_dot_exact_rhs(jnp.concatenate(
        [kk[p] * kk[p], r[p] * k2[p] * pvec_ref[2:3, lanes_of[p]]], axis=0), seg, WKV_SUM_TERMS)
        for p in pairs]
    kk = [kk[p] / jnp.maximum(jnp.sqrt(sums[p][0:C]), KK_EPS) for p in pairs]
    bonus = [sums[p][C:2 * C] for p in pairs]
    if t_real is not None:
        live = (c * C + trow) < t_real
        lw = [jnp.where(live, z, 0.0) for z in lw]
        kk = [jnp.where(live, z, 0.0) for z in kk]
        k2 = [jnp.where(live, z, 0.0) for z in k2]
        v = [jnp.where(live, z, 0.0) for z in v]
    bv = [kk[p] * a[p] for p in pairs]

    logp = [_dot_exact_lhs(tri, lw[p], WKV_SUM_TERMS) for p in pairs]
    logpc = [z[C - 1:C, :] for z in logp]
    lhs, rhs, vst, bk = [], [], [], []
    for p in pairs:
        inv_p = jnp.exp(-logp[p])
        tail = jnp.exp(logpc[p] - logp[p])
        a_t = -kk[p] * jnp.exp(logp[p] - lw[p])
        r_t = r[p] * jnp.exp(logp[p])
        lhs.append(jnp.concatenate([a_t, r_t], axis=0).astype(BF16))
        rhs.append(jnp.concatenate([stack(bv[p] * inv_p), stack(k2[p] * inv_p)], axis=0).astype(BF16))
        bk.append(jnp.concatenate([stack(bv[p] * tail), stack(k2[p] * tail)], axis=0).astype(BF16))
        vst.append(stack(v[p]).astype(BF16))

    quad = [_dot_nt(lhs[p], rhs[p]) for p in pairs]
    sv = [sv_ref[p] for p in pairs]
    ss = [_dot_nt(lhs[p], sv[p].astype(BF16)) for p in pairs]
    twice = lambda z: jnp.concatenate([z, z], axis=0)
    n_pow = [jnp.where(strict, twice(quad[p][0:C, 0:G]), 0.0).astype(BF16) for p in pairs]
    a_ak = [jnp.where(strict, twice(quad[p][0:C, G:2 * G]), 0.0).astype(BF16) for p in pairs]
    a_r = [jnp.where(incl_row, quad[p][C:2 * C, :], 0.0).astype(BF16) for p in pairs]

    u = [stack(ss[p][0:C]) + _dot(a_ak[p], vst[p]) for p in pairs]
    span = 1
    while 2 * span < C:
        both = [_dot(n_pow[p], jnp.concatenate([u[p].astype(BF16), n_pow[p]], axis=1)) for p in pairs]
        u = [u[p] + both[p][:, 0:LANES] for p in pairs]
        n_pow = [both[p][:, LANES:LANES + G].astype(BF16) for p in pairs]
        span *= 2
    u = [u[p] + _dot(n_pow[p], u[p].astype(BF16)) for p in pairs]
    uv = [jnp.concatenate([u[p].astype(BF16), vst[p]], axis=0) for p in pairs]
    o = [ss[p][C:2 * C] + _dot(a_r[p], uv[p]) for p in pairs]
    for p in pairs:
        sv_ref[p] = sv[p] * jnp.exp(logpc[p]) + _dot_tn(uv[p], bk[p])

    d = [o[p] - _dot_exact_rhs(o[p], seg, WKV_SUM_TERMS) * (1.0 / head) for p in pairs]
    var = [_dot_exact_rhs(d[p] * d[p], seg, WKV_SUM_TERMS) * (1.0 / head) for p in pairs]
    for p in pairs:
        sl = lanes_of[p]
        y = d[p] * lax.rsqrt(var[p] + GN_EPS) * pvec_ref[3:4, sl] + pvec_ref[4:5, sl]
        y = y + bonus[p] * v[p]
        o_ref[0, :, sl] = (y * g_ref[0, :, sl]).astype(o_ref.dtype)

    @pl.when(c == n_chunks - 1)
    def _():
        for p in range(n_pairs):
            for h in range(2):
                rows = sv_ref[p, h * head:(h + 1) * head, :]
                sout_ref[0, 2 * p + h] = _dot_nt(*_pair3(rows, place[h]))


def _pair3(x, e):
    h1, h2, h3 = _split3(x)
    return jnp.concatenate([h1, h2, h3], axis=1), jnp.concatenate([e, e, e], axis=1)


def _wkv(y_rkv, y_prev, lw, a, g, mu_rkv, pvec, state0, t_real):
    B, t_in, RW3 = y_rkv.shape
    RW = RW3 // 3
    head = LANES // 2
    NH = RW // head
    C = WKV_CHUNK
    T = _round_up(t_in, C)
    if T != t_in:
        pad_t = lambda z: jnp.pad(z, ((0, 0), (0, T - t_in), (0, 0)))
        y_rkv, lw, a, g = pad_t(y_rkv), pad_t(lw), pad_t(a), pad_t(g)
    n_chunks = T // C
    n_pairs = _pick_tile(RW // LANES, (8, 4, 2, 1))
    lwd = n_pairs * LANES
    ncol = RW // lwd
    G = 2 * C
    tri = (jnp.arange(C)[:, None] >= jnp.arange(C)[None, :]).astype(BF16)
    seg = ((jnp.arange(LANES)[:, None] // head) == (jnp.arange(LANES)[None, :] // head)).astype(BF16)

    def cols(part):
        return pl.BlockSpec((1, C, lwd), lambda b, j, c: (b, c, j + part * ncol))

    def prev_cols(part):
        return pl.BlockSpec((1, 1, lwd), lambda b, j, c: (b, 0, j + part * ncol))

    tile = pl.BlockSpec((1, C, lwd), lambda b, j, c: (b, c, j))
    in_specs = [cols(0), cols(1), cols(2), prev_cols(0), prev_cols(1), prev_cols(2), tile, tile, tile,
                pl.BlockSpec((3, lwd), lambda b, j, c: (0, j)),
                pl.BlockSpec((SUBLANES, lwd), lambda b, j, c: (0, j)),
                pl.BlockSpec((C, C), lambda b, j, c: (0, 0)),
                pl.BlockSpec((LANES, LANES), lambda b, j, c: (0, 0))]
    args = [y_rkv, y_rkv, y_rkv, y_prev, y_prev, y_prev, lw, a, g, mu_rkv, pvec, tri, seg]
    state_spec = pl.BlockSpec((1, 2 * n_pairs, head, head), lambda b, j, c: (b, j, 0, 0))
    if state0 is not None:
        in_specs.append(state_spec)
        args.append(state0)
    kern = functools.partial(_wkv_kernel, has_state=state0 is not None, n_pairs=n_pairs,
                             t_real=None if t_real == T else t_real, n_chunks=n_chunks, head=head)
    o, state = pl.pallas_call(
        kern,
        grid=(B, ncol, n_chunks),
        in_specs=in_specs,
        out_specs=[tile, state_spec],
        out_shape=[jax.ShapeDtypeStruct((B, T, RW), BF16),
                   jax.ShapeDtypeStruct((B, NH, head, head), F32)],
        scratch_shapes=[pltpu.VMEM((n_pairs, LANES, LANES), F32), pltpu.VMEM((SUBLANES, lwd), F32)],
        compiler_params=_cparams("arbitrary", "arbitrary", "arbitrary"),
    )(*args)
    return o[:, :t_in], state


def _rope_tables(pos, head_dim):
    half = head_dim // 2
    inv_freq = ROPE_THETA ** (-jnp.arange(half, dtype=F32) / half)
    ang = pos.astype(F32)[:, None] * inv_freq[None, :]
    cos, sin = jnp.cos(ang), jnp.sin(ang)
    return jnp.concatenate([cos, cos], axis=1), jnp.concatenate([-sin, sin], axis=1)


def _pad_cols(w, mult):
    pad = _round_up(w.shape[-1], mult) - w.shape[-1]
    return jnp.pad(w, [(0, 0)] * (w.ndim - 1) + [(0, pad)]) if pad else w


def _pad_rows(w, mult):
    pad = _round_up(w.shape[-2], mult) - w.shape[-2]
    return jnp.pad(w, [(0, 0)] * (w.ndim - 2) + [(0, pad), (0, 0)]) if pad else w


def kernel(x_prompt, x_sample, cache_k, cache_v, state_wkv, state_shift, page_table, norm_mix, norm_ffn, norm_final, w_in, w_out, mu_rkv, mu_wag, decay_w0, decay_w1, decay_w2, aaa_a0, aaa_a1, aaa_a2, gate_g1, gate_g2, k_k, k_a, r_k, ln_x_w, ln_x_b, ffn_w_gate, ffn_w_up, ffn_w_down):
    depth = w_in.shape[0]
    D = x_prompt.shape[-1]
    n_att_heads, head_dim = cache_k.shape[3], cache_k.shape[4]
    att_w = n_att_heads * head_dim
    rw = mu_rkv.shape[-1]
    n_rwkv_heads, rwkv_head = r_k.shape[1], r_k.shape[2]
    assert rwkv_head * 2 == LANES and head_dim == LANES
    Bp, Tp, _ = x_prompt.shape
    Bs, Ts, _ = x_sample.shape
    Tsp = _round_up(Ts, SUBLANES)
    past_len = page_table.shape[1] * cache_k.shape[2]

    wd_b = ffn_w_down.astype(BF16)
    w1_b, a1_b = _pad_cols(decay_w1, LANES).astype(BF16), _pad_cols(aaa_a1, LANES).astype(BF16)
    g1_b = _pad_cols(gate_g1, LANES).astype(BF16)
    w2_b, a2_b = _pad_rows(decay_w2, LANES).astype(BF16), _pad_rows(aaa_a2, LANES).astype(BF16)
    g2_b = _pad_rows(gate_g2, LANES).astype(BF16)

    cos_p, sin_p = _rope_tables(jnp.arange(Tp, dtype=jnp.int32), head_dim)
    cos_s, sin_s = _rope_tables(past_len + jnp.arange(Tsp, dtype=jnp.int32), head_dim)

    x_s = jnp.pad(x_sample, ((0, 0), (0, Tsp - Ts), (0, 0)))
    groups = [dict(x=x_prompt.reshape(Bp * Tp, D), B=Bp, T=Tp, t_real=Tp),
              dict(x=x_s.reshape(Bs * Tsp, D), B=Bs, T=Tsp, t_real=Ts)]
    outs = [dict(k=[], v=[], wkv=[], shift=[]) for _ in groups]
    Ms = Bs * Tsp
    cos_st, sin_st = jnp.tile(cos_s, (Bs, 1)), jnp.tile(sin_s, (Bs, 1))
    bf16_rows = 2 * SUBLANES

    for layer in range(depth):
        pvec = jnp.stack([k_k[layer], k_a[layer], r_k[layer].reshape(-1), ln_x_w[layer], ln_x_b[layer]]
                         + [jnp.zeros((rw,), F32)] * (SUBLANES - 5))
        sprevs = [jnp.zeros((Bp, 1, D), F32), state_shift[layer][:, None, :]]
        pro = [_prologue(grp["x"].reshape(grp["B"], grp["T"], D), sprev, norm_mix[layer][None], mu_wag[layer],
                         w1_b[layer], a1_b[layer], g1_b[layer], w2_b[layer], a2_b[layer], g2_b[layer],
                         decay_w0[layer][None], aaa_a0[layer][None], grp["t_real"])
               for grp, sprev in zip(groups, sprevs)]
        xn = [p[0].reshape(grp["B"] * grp["T"], D) for p, grp in zip(pro, groups)]

        rope = dict(extra_p=(cos_p, sin_p), extra_s=(cos_st, sin_st), rows_per_seq=Tp, head_dim=head_dim)
        q = _ws_matmul([xn[0]], [xn[1]], [w_in], layer, 0, att_w, "rope", F32, **rope)
        k = _ws_matmul([xn[0]], [xn[1]], [w_in], layer, att_w, att_w, "rope", F32, **rope)
        v = _ws_matmul([xn[0]], [xn[1]], [w_in], layer, 2 * att_w, att_w, "plain", F32)
        shift_rows = _pad_rows(state_shift[layer], bf16_rows).astype(BF16)
        rkv = _ws_matmul([xn[0]], [jnp.concatenate([xn[1], shift_rows], axis=0)], [w_in], layer,
                         3 * att_w, 3 * rw, "plain", F32)
        y_rkv = [rkv[0], rkv[1][:Ms]]
        y_prev = [jnp.zeros((Bp, 1, 3 * rw), F32), rkv[1][Ms:Ms + Bs].reshape(Bs, 1, 3 * rw)]
        state0 = [None, state_wkv[layer]]

        o_att, o_rwkv = [], []
        for gi, (grp, out) in enumerate(zip(groups, outs)):
            B, T, tr = grp["B"], grp["T"], grp["t_real"]
            q3, k3, v3 = (z[gi].reshape(B, T, att_w) for z in (q, k, v))
            if gi == 0:
                o_att.append(_moba_prompt(q3, k3, v3, head_dim))
            else:
                o_att.append(_moba_sample_gather(q3, k3, v3, cache_k, cache_v, page_table, layer, tr))
            _, lw, a, g, shift = pro[gi]
            o, wkv_new = _wkv(y_rkv[gi].reshape(B, T, 3 * rw), y_prev[gi], lw, a, g, mu_rkv[layer], pvec,
                              state0[gi], tr)
            o_rwkv.append(o)
            out["k"].append(k3[:, :tr].reshape(B, tr, n_att_heads, head_dim))
            out["v"].append(v3[:, :tr].reshape(B, tr, n_att_heads, head_dim))
            out["wkv"].append(wkv_new)
            out["shift"].append(shift.reshape(B, D))

        rows = lambda z, grp: z.reshape(grp["B"] * grp["T"], -1)
        h = _ws_matmul([rows(o_att[0], groups[0]), rows(o_rwkv[0], groups[0])],
                       [rows(o_att[1], groups[1]), rows(o_rwkv[1], groups[1])],
                       [w_out], layer, 0, D, "residual", F32,
                       extra_p=(groups[0]["x"],), extra_s=(groups[1]["x"],))
        hn = [_rmsnorm(z, norm_ffn[layer][None], BF16) for z in h]
        act = _ws_matmul([hn[0]], [hn[1]], [ffn_w_gate, ffn_w_up], layer, 0, ffn_w_gate.shape[2], "swiglu", BF16)
        for gi, grp in enumerate(groups):
            grp["x"] = _matmul_residual([act[gi]], wd_b, layer, h[gi])

    ys = []
    for grp in groups:
        B, T = grp["B"], grp["T"]
        y = _rmsnorm(grp["x"], norm_final[None], F32).reshape(B, T, D)
        ys.append(y[:, :grp["t_real"]])
    po, so = outs
    return (ys[0], ys[1],
            jnp.stack(po["k"]), jnp.stack(po["v"]), jnp.stack(po["wkv"]), jnp.stack(po["shift"]),
            jnp.stack(so["k"]), jnp.stack(so["v"]), jnp.stack(so["wkv"]), jnp.stack(so["shift"]))
```

```python
import functools
import math

import jax
import jax.numpy as jnp
from jax import lax
from jax.experimental import pallas as pl
from jax.experimental.pallas import tpu as pltpu

MOBA_BLOCK = 256
MOBA_TOPK = 3
ROPE_THETA = 10000.0
RMS_EPS = 1e-6
GN_EPS = 64e-5
KK_EPS = 1e-12
DECAY_SCALE = math.exp(-0.5)

LANES = 128
SUBLANES = 8
WKV_CHUNK = 64
WKV_SUM_TERMS = 2
VMEM_LIMIT = 56 * 1024 * 1024

F32 = jnp.float32
BF16 = jnp.bfloat16
NEG_INF = float("-inf")


def _cparams(*sem):
    return pltpu.CompilerParams(dimension_semantics=sem, vmem_limit_bytes=VMEM_LIMIT)


def _round_up(n, m):
    return -(-n // m) * m


def _pick_tile(n, prefs):
    for p in prefs:
        if n % p == 0:
            return p
    return n


def _split3(x):
    h1 = x.astype(BF16)
    r1 = x - h1.astype(F32)
    h2 = r1.astype(BF16)
    h3 = (r1 - h2.astype(F32)).astype(BF16)
    return h1, h2, h3


def _dot(a, b):
    return jnp.dot(a, b, preferred_element_type=F32)


def _dot_nt(a, b):
    return lax.dot_general(a, b, (((1,), (1,)), ((), ())), preferred_element_type=F32)


def _dot_tn(a, b):
    return lax.dot_general(a, b, (((0,), (0,)), ((), ())), preferred_element_type=F32)


def _dot_exact_rhs(x, e, terms=3):
    m = x.shape[0]
    y = _dot(jnp.concatenate(_split3(x)[:terms], axis=0), e)
    return sum(y[i * m:(i + 1) * m] for i in range(1, terms)) + y[0:m]


def _dot_exact_lhs(e, x, terms=3):
    n = x.shape[1]
    y = _dot(e, jnp.concatenate(_split3(x)[:terms], axis=1))
    return sum(y[:, i * n:(i + 1) * n] for i in range(1, terms)) + y[:, 0:n]


def _prologue_kernel(x_ref, sprev_ref, g_ref, mu_ref, w1_ref, a1_ref, g1_ref, w2_ref, a2_ref,
                     g2_ref, w0_ref, a0_ref, xn_ref, lw_ref, a_ref, gate_ref, shift_ref,
                     carry_ref, *, last_tile, last_row):
    i = pl.program_id(1)
    x = x_ref[0]
    tm = x.shape[0]
    xn = x * lax.rsqrt(jnp.mean(x * x, axis=-1, keepdims=True) + RMS_EPS) * g_ref[...]

    @pl.when(i == 0)
    def _():
        carry_ref[...] = sprev_ref[0]

    row = lax.broadcasted_iota(jnp.int32, xn.shape, 0)
    prev = jnp.where(row == 0, carry_ref[...], pltpu.roll(xn, 1, axis=0))
    carry_ref[...] = xn[tm - 1:tm, :]
    dx = prev - xn
    xw = (xn + dx * mu_ref[0:1, :]).astype(BF16)
    xa = (xn + dx * mu_ref[1:2, :]).astype(BF16)
    xg = (xn + dx * mu_ref[2:3, :]).astype(BF16)

    hw = jnp.tanh(_dot(xw, w1_ref[...])).astype(BF16)
    zw = w0_ref[...] + _dot(hw, w2_ref[...])
    lw_ref[0] = -DECAY_SCALE * jax.nn.sigmoid(zw)
    ha = _dot(xa, a1_ref[...]).astype(BF16)
    a_ref[0] = jax.nn.sigmoid(a0_ref[...] + _dot(ha, a2_ref[...]))
    hg = jax.nn.sigmoid(_dot(xg, g1_ref[...])).astype(BF16)
    gate_ref[0] = _dot(hg, g2_ref[...])
    xn_ref[0] = xn.astype(BF16)

    @pl.when(i == last_tile)
    def _():
        shift_ref[0] = xn[last_row:last_row + 1, :]


def _prologue(x, sprev, g, mu, w1, a1, g1, w2, a2, g2, w0, a0, t_real):
    B, T, D = x.shape
    RW = w2.shape[1]
    tm = _pick_tile(T, (128, 64, 32, 16, 8))
    nt = T // tm
    full = lambda arr: pl.BlockSpec(arr.shape, lambda b, i: (0,) * arr.ndim)
    row_spec = lambda w: pl.BlockSpec((1, tm, w), lambda b, i: (b, i, 0))
    kern = functools.partial(_prologue_kernel, last_tile=(t_real - 1) // tm,
                             last_row=(t_real - 1) % tm)
    return pl.pallas_call(
        kern,
        grid=(B, nt),
        in_specs=[row_spec(D), pl.BlockSpec((1, 1, D), lambda b, i: (b, 0, 0)), full(g), full(mu),
                  full(w1), full(a1), full(g1), full(w2), full(a2), full(g2), full(w0), full(a0)],
        out_specs=[row_spec(D), row_spec(RW), row_spec(RW), row_spec(RW),
                   pl.BlockSpec((1, 1, D), lambda b, i: (b, 0, 0))],
        out_shape=[jax.ShapeDtypeStruct((B, T, D), BF16), jax.ShapeDtypeStruct((B, T, RW), F32),
                   jax.ShapeDtypeStruct((B, T, RW), F32), jax.ShapeDtypeStruct((B, T, RW), F32),
                   jax.ShapeDtypeStruct((B, 1, D), F32)],
        scratch_shapes=[pltpu.VMEM((1, D), F32)],
        compiler_params=_cparams("arbitrary", "arbitrary"),
    )(x, sprev, g, mu, w1, a1, g1, w2, a2, g2, w0, a0)


def _rmsnorm_kernel(x_ref, g_ref, o_ref):
    x = x_ref[...]
    y = x * lax.rsqrt(jnp.mean(x * x, axis=-1, keepdims=True) + RMS_EPS) * g_ref[...]
    o_ref[...] = y.astype(o_ref.dtype)


def _rmsnorm(x2d, g, out_dtype):
    M, D = x2d.shape
    tm = _pick_tile(M, (256, 128, 64, 32, 16, 8))
    return pl.pallas_call(
        _rmsnorm_kernel,
        grid=(M // tm,),
        in_specs=[pl.BlockSpec((tm, D), lambda i: (i, 0)), pl.BlockSpec((1, D), lambda i: (0, 0))],
        out_specs=pl.BlockSpec((tm, D), lambda i: (i, 0)),
        out_shape=jax.ShapeDtypeStruct((M, D), out_dtype),
        compiler_params=_cparams("arbitrary"),
    )(x2d, g)


_WS_EXTRAS = {"plain": 0, "rope": 2, "swiglu": 0, "residual": 1}


def _ws_kernel(*refs, n_lhs, n_grp, mode, head_dim):
    it = iter(refs)
    lhs_p = [next(it) for _ in range(n_lhs)]
    lhs_s = [next(it) for _ in range(n_lhs)]
    w = [[next(it) for _ in range(n_lhs)] for _ in range(n_grp)]
    extra_p = [next(it) for _ in range(_WS_EXTRAS[mode])]
    extra_s = [next(it) for _ in range(_WS_EXTRAS[mode])]
    o_p, o_s = next(it), next(it)
    wb = [[next(it) for _ in range(n_lhs)] for _ in range(n_grp)]

    def emit(lhs, extra, o_ref):
        ys = []
        for g in range(n_grp):
            y = _dot(lhs[0][...], wb[g][0][...])
            for t in range(1, n_lhs):
                y = y + _dot(lhs[t][...], wb[g][t][...])
            ys.append(y)
        if mode == "plain":
            o_ref[...] = ys[0]
        elif mode == "residual":
            o_ref[...] = extra[0][...] + ys[0]
        elif mode == "swiglu":
            o_ref[...] = (ys[0] * jax.nn.sigmoid(ys[0]) * ys[1]).astype(o_ref.dtype)
        else:
            cos, sin = extra[0][...], extra[1][...]
            for h in range(ys[0].shape[1] // head_dim):
                yh = ys[0][:, h * head_dim:(h + 1) * head_dim]
                o_ref[:, h * head_dim:(h + 1) * head_dim] = (
                    yh * cos + pltpu.roll(yh, head_dim // 2, axis=1) * sin)

    @pl.when(pl.program_id(1) == 0)
    def _():
        for g in range(n_grp):
            for t in range(n_lhs):
                wb[g][t][...] = w[g][t][...].astype(BF16)
        emit(lhs_s, extra_s, o_s)

    emit(lhs_p, extra_p, o_p)


def _ws_matmul(lhs_p, lhs_s, weights, layer, col_off, n_cols, mode, out_dtype, extra_p=(), extra_s=(),
               rows_per_seq=None, head_dim=LANES):
    n_lhs, n_grp = len(lhs_p), len(weights)
    M, kblk = lhs_p[0].shape
    Ms = lhs_s[0].shape[0]
    tm = _pick_tile(math.gcd(M, rows_per_seq) if rows_per_seq else M, (1024, 512, 256, 128, 64, 32, 16, 8))
    out_bytes = jnp.dtype(out_dtype).itemsize
    for tn_pref in (512, 256, 128):
        tn = _pick_tile(math.gcd(n_cols, col_off) if col_off else n_cols,
                        tuple(p for p in (512, 256, 128) if p <= tn_pref))
        w_elems = n_grp * n_lhs * kblk * tn
        need = (w_elems * (2 * 4 + 2) + 2 * n_lhs * kblk * (tm + Ms) * 2
                + 2 * (tm + Ms) * tn * (out_bytes + (4 if mode == "residual" else 0)))
        if need <= VMEM_LIMIT * 7 // 8:
            break
    off = col_off // tn
    row_p = lambda width: pl.BlockSpec((tm, width), lambda j, i: (i, 0))
    all_s = lambda width: pl.BlockSpec((Ms, width), lambda j, i: (0, 0))
    in_specs = [row_p(kblk)] * n_lhs + [all_s(kblk)] * n_lhs
    in_specs += [pl.BlockSpec((None, kblk, tn), functools.partial(lambda j, i, t: (layer, t, j + off), t=t))
                 for _ in range(n_grp) for t in range(n_lhs)]
    args = list(lhs_p) + list(lhs_s) + [wt for wt in weights for _ in range(n_lhs)]
    if mode == "rope":
        per = rows_per_seq // tm
        in_specs += [pl.BlockSpec((tm, head_dim), lambda j, i: (i % per, 0))] * 2 + [all_s(head_dim)] * 2
    elif mode == "residual":
        in_specs += [pl.BlockSpec((tm, tn), lambda j, i: (i, j)), pl.BlockSpec((Ms, tn), lambda j, i: (0, j))]
    args += list(extra_p) + list(extra_s)
    return pl.pallas_call(
        functools.partial(_ws_kernel, n_lhs=n_lhs, n_grp=n_grp, mode=mode, head_dim=head_dim),
        grid=(n_cols // tn, M // tm),
        in_specs=in_specs,
        out_specs=[pl.BlockSpec((tm, tn), lambda j, i: (i, j)), pl.BlockSpec((Ms, tn), lambda j, i: (0, j))],
        out_shape=[jax.ShapeDtypeStruct((M, n_cols), out_dtype), jax.ShapeDtypeStruct((Ms, n_cols), out_dtype)],
        scratch_shapes=[pltpu.VMEM((kblk, tn), BF16) for _ in range(n_grp * n_lhs)],
        compiler_params=_cparams("arbitrary", "arbitrary"),
    )(*args)


def _mm_res_kernel(*refs, n_lhs):
    res_ref = refs[2 * n_lhs]
    o_ref = refs[2 * n_lhs + 1]
    acc = res_ref[...]
    for t in range(n_lhs):
        acc = acc + _dot(refs[t][...], refs[n_lhs + t][...])
    o_ref[...] = acc


def _matmul_residual(lhs_list, w_all, layer, res2d):
    M, N = res2d.shape
    n_lhs = len(lhs_list)
    ks = [l.shape[1] for l in lhs_list]
    kblk = ks[0]
    assert all(k == kblk for k in ks)
    for tm_pref, tn_pref in ((1024, 512), (512, 512), (512, 256), (256, 256), (256, 128)):
        tm = _pick_tile(M, tuple(p for p in (1024, 512, 256, 128, 64, 32, 16, 8) if p <= tm_pref))
        tn = _pick_tile(N, tuple(p for p in (512, 256, 128) if p <= tn_pref))
        if 2 * (n_lhs * kblk * (tm + tn) * 2 + 2 * tm * tn * 4) <= VMEM_LIMIT * 3 // 4:
            break
    in_specs = [pl.BlockSpec((tm, kblk), lambda i, j: (i, 0)) for _ in lhs_list]
    in_specs += [pl.BlockSpec((None, kblk, tn), functools.partial(lambda i, j, t: (layer, t, j), t=t))
                 for t in range(n_lhs)]
    in_specs += [pl.BlockSpec((tm, tn), lambda i, j: (i, j))]
    return pl.pallas_call(
        functools.partial(_mm_res_kernel, n_lhs=n_lhs),
        grid=(M // tm, N // tn),
        in_specs=in_specs,
        out_specs=pl.BlockSpec((tm, tn), lambda i, j: (i, j)),
        out_shape=jax.ShapeDtypeStruct((M, N), F32),
        compiler_params=_cparams("arbitrary", "arbitrary"),
    )(*lhs_list, *([w_all] * n_lhs), res2d)


def _rank_select(g, n_rows, row_of):
    row = lax.broadcasted_iota(jnp.int32, g.shape, 0)

    def body(m, cnt):
        gm = row_of(m)
        better = (gm > g) | ((gm == g) & (m < row))
        return cnt + better.astype(jnp.int32)

    cnt = lax.fori_loop(0, n_rows, body, jnp.zeros(g.shape, jnp.int32))
    return cnt < MOBA_TOPK


def _moba_prompt_kernel(q_ref, k_ref, v_ref, o_ref, means_ref, kb_ref, vt_ref, gate_ref, sel_ref,
                        *, nb, scale, n_heads, head_dim):
    qi = pl.program_id(2)
    blk = MOBA_BLOCK
    heads = range(n_heads)
    cols = [slice(h * head_dim, (h + 1) * head_dim) for h in heads]

    @pl.when(qi == 0)
    def _():
        means_ref[...] = jnp.zeros(means_ref.shape, F32)
        for h in heads:
            for n in range(nb):
                kn = k_ref[0, n * blk:(n + 1) * blk, cols[h]]
                means_ref[h, n:n + 1, :] = jnp.mean(kn, axis=0, keepdims=True)
                kb_ref[h, n] = kn.astype(BF16)
                vt_ref[h, n] = v_ref[0, n * blk:(n + 1) * blk, cols[h]].T.astype(BF16)

    q = [q_ref[0, :, cols[h]] for h in heads]
    qb = [z.astype(BF16) for z in q]

    for h in heads:
        mh, ml, _ = _split3(means_ref[h])
        qh, ql, _ = _split3(q[h])
        gate = _dot_nt(mh, qh) + _dot_nt(mh, ql) + _dot_nt(ml, qh)
        row = lax.broadcasted_iota(jnp.int32, gate.shape, 0)
        gate_ref[h] = jnp.where(row < qi, gate, NEG_INF)
    for h in heads:
        g = gate_ref[h]
        row = lax.broadcasted_iota(jnp.int32, g.shape, 0)
        sel = (row < qi) & _rank_select(g, nb, lambda m, h=h: gate_ref[h, pl.ds(m, 1), :])
        sel_f = sel.astype(F32)
        for n in range(nb):
            sel_ref[h, n] = sel_f[n:n + 1, :]

    s = [_dot_nt(kb_ref[h, qi], qb[h]) * scale for h in heads]
    kidx = lax.broadcasted_iota(jnp.int32, s[0].shape, 0)
    qidx = lax.broadcasted_iota(jnp.int32, s[0].shape, 1)
    s = [jnp.where(kidx <= qidx, z, NEG_INF) for z in s]
    m0 = [jnp.max(z, axis=0, keepdims=True) for z in s]
    p = [jnp.exp(s[h] - m0[h]) for h in heads]
    l0 = [jnp.sum(z, axis=0, keepdims=True) for z in p]
    acc0 = [_dot(vt_ref[h, qi], p[h].astype(BF16)) for h in heads]

    def body(n, carry):
        m, l, acc = carry
        s = [_dot_nt(kb_ref[h, n], qb[h]) * scale for h in heads]
        s = [jnp.where(sel_ref[h, n] > 0.0, s[h], NEG_INF) for h in heads]
        m_new = [jnp.maximum(m[h], jnp.max(s[h], axis=0, keepdims=True)) for h in heads]
        alpha = [jnp.exp(m[h] - m_new[h]) for h in heads]
        p = [jnp.exp(s[h] - m_new[h]) for h in heads]
        l = [l[h] * alpha[h] + jnp.sum(p[h], axis=0, keepdims=True) for h in heads]
        pv = [_dot(vt_ref[h, n], p[h].astype(BF16)) for h in heads]
        acc = [acc[h] * alpha[h] + pv[h] for h in heads]
        return m_new, l, acc

    _, l, acc = lax.fori_loop(0, qi, body, (m0, l0, acc0))
    for h in heads:
        o_ref[0, :, cols[h]] = (acc[h] / l[h]).T.astype(o_ref.dtype)


MOBA_HEADS_PER_STEP = 8
MOBA_ROUTE_BLOCKS_PER_STEP = 4


def _moba_prompt(q, k, v, head_dim):
    B, T, W = q.shape
    assert T % MOBA_BLOCK == 0 and head_dim == LANES
    H = W // head_dim
    hp = _pick_tile(H, (MOBA_HEADS_PER_STEP, 1))
    wp = hp * head_dim
    nb = T // MOBA_BLOCK
    nbp = _round_up(nb, SUBLANES)
    kern = functools.partial(_moba_prompt_kernel, nb=nb, scale=head_dim ** -0.5, n_heads=hp,
                             head_dim=head_dim)
    kv_spec = pl.BlockSpec((1, T, wp), lambda b, h, i: (b, 0, h))
    return pl.pallas_call(
        kern,
        grid=(B, H // hp, nb),
        in_specs=[pl.BlockSpec((1, MOBA_BLOCK, wp), lambda b, h, i: (b, i, h)), kv_spec, kv_spec],
        out_specs=pl.BlockSpec((1, MOBA_BLOCK, wp), lambda b, h, i: (b, i, h)),
        out_shape=jax.ShapeDtypeStruct((B, T, W), BF16),
        scratch_shapes=[pltpu.VMEM((hp, nbp, head_dim), F32),
                        pltpu.VMEM((hp, nb, MOBA_BLOCK, head_dim), BF16),
                        pltpu.VMEM((hp, nb, head_dim, MOBA_BLOCK), BF16),
                        pltpu.VMEM((hp, nbp, MOBA_BLOCK), F32),
                        pltpu.VMEM((hp, nbp, 1, MOBA_BLOCK), F32)],
        compiler_params=_cparams("arbitrary", "arbitrary", "arbitrary"),
    )(q, k, v)


def _moba_route_kernel(pt_ref, q_ref, *rest, ppb, nbk, n_heads):
    n_pages = len(rest) - 2
    k_refs = rest[:n_pages]
    ids_ref, gate_ref = rest[n_pages:]
    n = pl.program_id(1)
    dh = k_refs[0].shape[1]
    bps = n_pages // ppb
    for bb in range(bps):
        tot = jnp.zeros((n_heads, dh), F32)
        for r in k_refs[bb * ppb:(bb + 1) * ppb]:
            tot = tot + jnp.sum(r[...].reshape(-1, n_heads, dh), axis=0)
        mean = tot * (1.0 / MOBA_BLOCK)
        gate_ref[n * bps + bb] = jnp.sum(q_ref[0] * mean[None], axis=-1)

    @pl.when(n == nbk // bps - 1)
    def _():
        g = gate_ref[...]
        blk = lax.broadcasted_iota(jnp.int32, g.shape, 0)

        def body(m, cnt):
            gm = gate_ref[m][None]
            better = (gm > g) | ((gm == g) & (m < blk))
            return cnt + better.astype(jnp.int32)

        rank = lax.fori_loop(0, nbk, body, jnp.zeros(g.shape, jnp.int32))
        for j in range(MOBA_TOPK):
            ids_ref[0, j] = jnp.sum(jnp.where(rank == j, blk, 0), axis=0)


def _moba_gather_kernel(pt_ref, ids_ref, q_ref, ko_ref, vo_ref, ck_ref, cv_ref, o_ref,
                        kbuf, vbuf, sem, *, layer, ppb, n_q, n_heads, scale):
    b = pl.program_id(0)
    n_slots = n_q * MOBA_TOPK
    page = kbuf.shape[3]
    ids_q = ids_ref.shape[1] // (MOBA_TOPK * n_heads)

    def copies(h, slot):
        out = []
        for q in range(n_q):
            for j in range(MOBA_TOPK):
                blk = ids_ref[b, (j * ids_q + q) * n_heads + h]
                for p in range(ppb):
                    pid = pt_ref[b, blk * ppb + p]
                    s = q * MOBA_TOPK + j
                    out.append(pltpu.make_async_copy(ck_ref.at[layer, pid, :, h, :], kbuf.at[slot, s, p],
                                                     sem.at[0, slot]))
                    out.append(pltpu.make_async_copy(cv_ref.at[layer, pid, :, h, :], vbuf.at[slot, s, p],
                                                     sem.at[1, slot]))
        return out

    for c in copies(0, 0):
        c.start()

    def head_body(h, carry):
        slot = lax.rem(h, 2)

        @pl.when(h + 1 < n_heads)
        def _():
            for c in copies(h + 1, 1 - slot):
                c.start()

        qb = q_ref[0, h].astype(BF16)
        s_own = _dot_nt(qb, ko_ref[0, h].astype(BF16)) * scale
        kidx = lax.broadcasted_iota(jnp.int32, s_own.shape, 1)
        qidx = lax.broadcasted_iota(jnp.int32, s_own.shape, 0)
        s_own = jnp.where(kidx <= qidx, s_own, NEG_INF)

        for c in copies(h, slot):
            c.wait()
        kb = kbuf[slot].reshape(n_slots * ppb * page, -1).astype(BF16)
        vb = vbuf[slot].reshape(n_slots * ppb * page, -1).astype(BF16)
        s = _dot_nt(qb, kb) * scale
        col_q = lax.broadcasted_iota(jnp.int32, s.shape, 1) // (MOBA_TOPK * ppb * page)
        row_q = lax.broadcasted_iota(jnp.int32, s.shape, 0)
        s = jnp.where(col_q == row_q, s, NEG_INF)
        m = jnp.maximum(jnp.max(s_own, axis=1, keepdims=True), jnp.max(s, axis=1, keepdims=True))
        p_own = jnp.exp(s_own - m)
        p = jnp.exp(s - m)
        l = jnp.sum(p_own, axis=1, keepdims=True) + jnp.sum(p, axis=1, keepdims=True)
        acc = _dot(p_own.astype(BF16), vo_ref[0, h].astype(BF16)) + _dot(p.astype(BF16), vb)
        o_ref[0, h] = (acc / l).astype(o_ref.dtype)
        return carry

    lax.fori_loop(0, n_heads, head_body, 0)


def _moba_sample_gather(q, k_new, v_new, cache_k, cache_v, page_table, layer, n_q):
    B, Ts, W = q.shape
    page, H, dh = cache_k.shape[2:]
    n_pages = page_table.shape[1]
    assert MOBA_BLOCK % page == 0 and (n_pages * page) % MOBA_BLOCK == 0 and dh == LANES
    ppb = MOBA_BLOCK // page
    nbk = n_pages // ppb
    assert nbk >= MOBA_TOPK
    qr = _round_up(n_q, SUBLANES)
    qp = _round_up(n_q, 2 * SUBLANES)
    own = LANES
    assert n_q <= Ts <= own
    pad_rows = lambda z, r: jnp.pad(z, ((0, 0), (0, r - z.shape[1]), (0, 0)))
    heads_first = lambda z: z.reshape(B, -1, H, dh).transpose(0, 2, 1, 3)
    q_route = pad_rows(q[:, :n_q], qr).reshape(B, qr, H, dh)
    q_h = heads_first(pad_rows(q[:, :n_q], qp))
    ko_h, vo_h = heads_first(pad_rows(k_new, own)), heads_first(pad_rows(v_new, own))
    cache_k_flat = cache_k.reshape(cache_k.shape[:2] + (page * H, dh))

    bps = _pick_tile(nbk, (MOBA_ROUTE_BLOCKS_PER_STEP, 2, 1))
    pps = ppb * bps

    def page_spec(j):
        return pl.BlockSpec((None, None, page * H, dh),
                            lambda b, n, pt: (layer, pt[b, n * pps + j], 0, 0))

    ids = pl.pallas_call(
        functools.partial(_moba_route_kernel, ppb=ppb, nbk=nbk, n_heads=H),
        grid_spec=pltpu.PrefetchScalarGridSpec(
            num_scalar_prefetch=1,
            grid=(B, nbk // bps),
            in_specs=[pl.BlockSpec((1, qr, H, dh), lambda b, n, pt: (b, 0, 0, 0))]
                     + [page_spec(j) for j in range(pps)],
            out_specs=pl.BlockSpec((1, MOBA_TOPK, qr, H), lambda b, n, pt: (b, 0, 0, 0)),
            scratch_shapes=[pltpu.VMEM((nbk, qr, H), F32)]),
        out_shape=jax.ShapeDtypeStruct((B, MOBA_TOPK, qr, H), jnp.int32),
        compiler_params=_cparams("arbitrary", "arbitrary"),
    )(page_table, q_route, *([cache_k_flat] * pps))

    n_slots = n_q * MOBA_TOPK
    per_head = lambda rows: pl.BlockSpec((1, H, rows, dh), lambda b, pt, ids: (b, 0, 0, 0))
    o_h = pl.pallas_call(
        functools.partial(_moba_gather_kernel, layer=layer, ppb=ppb, n_q=n_q, n_heads=H, scale=dh ** -0.5),
        grid_spec=pltpu.PrefetchScalarGridSpec(
            num_scalar_prefetch=2,
            grid=(B,),
            in_specs=[per_head(qp), per_head(own), per_head(own),
                      pl.BlockSpec(memory_space=pl.ANY), pl.BlockSpec(memory_space=pl.ANY)],
            out_specs=per_head(qp),
            scratch_shapes=[pltpu.VMEM((2, n_slots, ppb, page, dh), F32),
                            pltpu.VMEM((2, n_slots, ppb, page, dh), F32),
                            pltpu.SemaphoreType.DMA((2, 2))]),
        out_shape=jax.ShapeDtypeStruct((B, H, qp, dh), BF16),
        compiler_params=_cparams("arbitrary"),
    )(page_table, ids.reshape(B, -1), q_h, ko_h, vo_h, cache_k, cache_v)
    o = o_h.transpose(0, 2, 1, 3).reshape(B, qp, W)
    return pad_rows(o[:, :n_q], Ts)


def _wkv_kernel(*refs, has_state, n_pairs, t_real, n_chunks, head):
    it = iter(refs)
    yr_ref, yk_ref, yv_ref = next(it), next(it), next(it)
    pr_ref, pk_ref, pv_ref = next(it), next(it), next(it)
    lw_ref, a_ref, g_ref = next(it), next(it), next(it)
    mu_ref, pvec_ref = next(it), next(it)
    tri_ref, seg_ref = next(it), next(it)
    s0_ref = next(it) if has_state else None
    o_ref, sout_ref = next(it), next(it)
    sv_ref, carry_ref = next(it), next(it)

    c = pl.program_id(2)
    C = yr_ref.shape[1]
    G = 2 * C
    lane_g = lax.broadcasted_iota(jnp.int32, (G, LANES), 1)
    row_g = lax.broadcasted_iota(jnp.int32, (G, LANES), 0)
    stack_mask = (row_g // C) == (lane_g // head)
    rr = lax.broadcasted_iota(jnp.int32, (G, G), 0)
    cc = lax.broadcasted_iota(jnp.int32, (G, G), 1)
    strict = ((rr // C) == (cc // C)) & ((cc % C) < (rr % C))
    incl_row = (lax.broadcasted_iota(jnp.int32, (C, 2 * G), 1) % C
                <= lax.broadcasted_iota(jnp.int32, (C, 2 * G), 0))
    pr_i = lax.broadcasted_iota(jnp.int32, (head, LANES), 0)
    pc_i = lax.broadcasted_iota(jnp.int32, (head, LANES), 1)
    place = [(pc_i == pr_i + h * head).astype(BF16) for h in range(2)]
    seg = seg_ref[...]
    tri = tri_ref[...]
    trow = lax.broadcasted_iota(jnp.int32, (C, LANES), 0)

    def stack(x):
        return jnp.where(stack_mask, jnp.concatenate([x, x], axis=0), 0.0)

    @pl.when(c == 0)
    def _():
        carry_ref[0:1, :] = pr_ref[0]
        carry_ref[1:2, :] = pk_ref[0]
        carry_ref[2:3, :] = pv_ref[0]
        for p in range(n_pairs):
            if has_state:
                blocks = [_dot_exact_rhs(s0_ref[0, 2 * p + h], place[h]) for h in range(2)]
                sv_ref[p] = jnp.concatenate(blocks, axis=0)
            else:
                sv_ref[p] = jnp.zeros((LANES, LANES), F32)

    pairs = range(n_pairs)
    lanes_of = [slice(p * LANES, (p + 1) * LANES) for p in pairs]

    def mixed(y_ref, idx, sl):
        y = y_ref[0, :, sl]
        prev = jnp.where(trow == 0, carry_ref[idx:idx + 1, sl], pltpu.roll(y, 1, axis=0))
        carry_ref[idx:idx + 1, sl] = y[C - 1:C, :]
        return y + (prev - y) * mu_ref[idx:idx + 1, sl]

    r = [mixed(yr_ref, 0, sl) for sl in lanes_of]
    k = [mixed(yk_ref, 1, sl) for sl in lanes_of]
    v = [mixed(yv_ref, 2, sl) for sl in lanes_of]
    a = [a_ref[0, :, sl] for sl in lanes_of]
    lw = [lw_ref[0, :, sl] for sl in lanes_of]
    kk = [k[p] * pvec_ref[0:1, lanes_of[p]] for p in pairs]
    k2 = [k[p] * (1.0 + (a[p] - 1.0) * pvec_ref[1:2, lanes_of[p]]) for p in pairs]
    sums = [_dot_exact_rhs(jnp.concatenate(
        [kk[p] * kk[p], r[p] * k2[p] * pvec_ref[2:3, lanes_of[p]]], axis=0), seg, WKV_SUM_TERMS)
        for p in pairs]
    kk = [kk[p] / jnp.maximum(jnp.sqrt(sums[p][0:C]), KK_EPS) for p in pairs]
    bonus = [sums[p][C:2 * C] for p in pairs]
    if t_real is not None:
        live = (c * C + trow) < t_real
        lw = [jnp.where(live, z, 0.0) for z in lw]
        kk = [jnp.where(live, z, 0.0) for z in kk]
        k2 = [jnp.where(live, z, 0.0) for z in k2]
        v = [jnp.where(live, z, 0.0) for z in v]
    bv = [kk[p] * a[p] for p in pairs]

    logp = [_dot_exact_lhs(tri, lw[p], WKV_SUM_TERMS) for p in pairs]
    logpc = [z[C - 1:C, :] for z in logp]
    lhs, rhs, vst, bk = [], [], [], []
    for p in pairs:
        inv_p = jnp.exp(-logp[p])
        tail = jnp.exp(logpc[p] - logp[p])
        a_t = -kk[p] * jnp.exp(logp[p] - lw[p])
        r_t = r[p] * jnp.exp(logp[p])
        lhs.append(jnp.concatenate([a_t, r_t], axis=0).astype(BF16))
        rhs.append(jnp.concatenate([stack(bv[p] * inv_p), stack(k2[p] * inv_p)], axis=0).astype(BF16))
        bk.append(jnp.concatenate([stack(bv[p] * tail), stack(k2[p] * tail)], axis=0).astype(BF16))
        vst.append(stack(v[p]).astype(BF16))

    quad = [_dot_nt(lhs[p], rhs[p]) for p in pairs]
    sv = [sv_ref[p] for p in pairs]
    ss = [_dot_nt(lhs[p], sv[p].astype(BF16)) for p in pairs]
    twice = lambda z: jnp.concatenate([z, z], axis=0)
    n_pow = [jnp.where(strict, twice(quad[p][0:C, 0:G]), 0.0).astype(BF16) for p in pairs]
    a_ak = [jnp.where(strict, twice(quad[p][0:C, G:2 * G]), 0.0).astype(BF16) for p in pairs]
    a_r = [jnp.where(incl_row, quad[p][C:2 * C, :], 0.0).astype(BF16) for p in pairs]

    u = [stack(ss[p][0:C]) + _dot(a_ak[p], vst[p]) for p in pairs]
    span = 1
    while 2 * span < C:
        both = [_dot(n_pow[p], jnp.concatenate([u[p].astype(BF16), n_pow[p]], axis=1)) for p in pairs]
        u = [u[p] + both[p][:, 0:LANES] for p in pairs]
        n_pow = [both[p][:, LANES:LANES + G].astype(BF16) for p in pairs]
        span *= 2
    u = [u[p] + _dot(n_pow[p], u[p].astype(BF16)) for p in pairs]
    uv = [jnp.concatenate([u[p].astype(BF16), vst[p]], axis=0) for p in pairs]
    o = [ss[p][C:2 * C] + _dot(a_r[p], uv[p]) for p in pairs]
    for p in pairs:
        sv_ref[p] = sv[p] * jnp.exp(logpc[p]) + _dot_tn(uv[p], bk[p])

    d = [o[p] - _dot_exact_rhs(o[p], seg, WKV_SUM_TERMS) * (1.0 / head) for p in pairs]
    var = [_dot_exact_rhs(d[p] * d[p], seg, WKV_SUM_TERMS) * (1.0 / head) for p in pairs]
    for p in pairs:
        sl = lanes_of[p]
        y = d[p] * lax.rsqrt(var[p] + GN_EPS) * pvec_ref[3:4, sl] + pvec_ref[4:5, sl]
        y = y + bonus[p] * v[p]
        o_ref[0, :, sl] = (y * g_ref[0, :, sl]).astype(o_ref.dtype)

    @pl.when(c == n_chunks - 1)
    def _():
        for p in range(n_pairs):
            for h in range(2):
                rows = sv_ref[p, h * head:(h + 1) * head, :]
                sout_ref[0, 2 * p + h] = _dot_nt(*_pair3(rows, place[h]))


def _pair3(x, e):
    h1, h2, h3 = _split3(x)
    return jnp.concatenate([h1, h2, h3], axis=1), jnp.concatenate([e, e, e], axis=1)


def _wkv(y_rkv, y_prev, lw, a, g, mu_rkv, pvec, state0, t_real):
    B, t_in, RW3 = y_rkv.shape
    RW = RW3 // 3
    head = LANES // 2
    NH = RW // head
    C = WKV_CHUNK
    T = _round_up(t_in, C)
    if T != t_in:
        pad_t = lambda z: jnp.pad(z, ((0, 0), (0, T - t_in), (0, 0)))
        y_rkv, lw, a, g = pad_t(y_rkv), pad_t(lw), pad_t(a), pad_t(g)
    n_chunks = T // C
    n_pairs = _pick_tile(RW // LANES, (16, 8, 4, 2, 1))
    lwd = n_pairs * LANES
    ncol = RW // lwd
    G = 2 * C
    tri = (jnp.arange(C)[:, None] >= jnp.arange(C)[None, :]).astype(BF16)
    seg = ((jnp.arange(LANES)[:, None] // head) == (jnp.arange(LANES)[None, :] // head)).astype(BF16)

    def cols(part):
        return pl.BlockSpec((1, C, lwd), lambda b, j, c: (b, c, j + part * ncol))

    def prev_cols(part):
        return pl.BlockSpec((1, 1, lwd), lambda b, j, c: (b, 0, j + part * ncol))

    tile = pl.BlockSpec((1, C, lwd), lambda b, j, c: (b, c, j))
    in_specs = [cols(0), cols(1), cols(2), prev_cols(0), prev_cols(1), prev_cols(2), tile, tile, tile,
                pl.BlockSpec((3, lwd), lambda b, j, c: (0, j)),
                pl.BlockSpec((SUBLANES, lwd), lambda b, j, c: (0, j)),
                pl.BlockSpec((C, C), lambda b, j, c: (0, 0)),
                pl.BlockSpec((LANES, LANES), lambda b, j, c: (0, 0))]
    args = [y_rkv, y_rkv, y_rkv, y_prev, y_prev, y_prev, lw, a, g, mu_rkv, pvec, tri, seg]
    state_spec = pl.BlockSpec((1, 2 * n_pairs, head, head), lambda b, j, c: (b, j, 0, 0))
    if state0 is not None:
        in_specs.append(state_spec)
        args.append(state0)
    kern = functools.partial(_wkv_kernel, has_state=state0 is not None, n_pairs=n_pairs,
                             t_real=None if t_real == T else t_real, n_chunks=n_chunks, head=head)
    o, state = pl.pallas_call(
        kern,
        grid=(B, ncol, n_chunks),
        in_specs=in_specs,
        out_specs=[tile, state_spec],
        out_shape=[jax.ShapeDtypeStruct((B, T, RW), BF16),
                   jax.ShapeDtypeStruct((B, NH, head, head), F32)],
        scratch_shapes=[pltpu.VMEM((n_pairs, LANES, LANES), F32), pltpu.VMEM((SUBLANES, lwd), F32)],
        compiler_params=_cparams("arbitrary", "arbitrary", "arbitrary"),
    )(*args)
    return o[:, :t_in], state


def _rope_tables(pos, head_dim):
    half = head_dim // 2
    inv_freq = ROPE_THETA ** (-jnp.arange(half, dtype=F32) / half)
    ang = pos.astype(F32)[:, None] * inv_freq[None, :]
    cos, sin = jnp.cos(ang), jnp.sin(ang)
    return jnp.concatenate([cos, cos], axis=1), jnp.concatenate([-sin, sin], axis=1)


def _pad_cols(w, mult):
    pad = _round_up(w.shape[-1], mult) - w.shape[-1]
    return jnp.pad(w, [(0, 0)] * (w.ndim - 1) + [(0, pad)]) if pad else w


def _pad_rows(w, mult):
    pad = _round_up(w.shape[-2], mult) - w.shape[-2]
    return jnp.pad(w, [(0, 0)] * (w.ndim - 2) + [(0, pad), (0, 0)]) if pad else w


def kernel(x_prompt, x_sample, cache_k, cache_v, state_wkv, state_shift, page_table, norm_mix, norm_ffn, norm_final, w_in, w_out, mu_rkv, mu_wag, decay_w0, decay_w1, decay_w2, aaa_a0, aaa_a1, aaa_a2, gate_g1, gate_g2, k_k, k_a, r_k, ln_x_w, ln_x_b, ffn_w_gate, ffn_w_up, ffn_w_down):
    depth = w_in.shape[0]
    D = x_prompt.shape[-1]
    n_att_heads, head_dim = cache_k.shape[3], cache_k.shape[4]
    att_w = n_att_heads * head_dim
    rw = mu_rkv.shape[-1]
    n_rwkv_heads, rwkv_head = r_k.shape[1], r_k.shape[2]
    assert rwkv_head * 2 == LANES and head_dim == LANES
    Bp, Tp, _ = x_prompt.shape
    Bs, Ts, _ = x_sample.shape
    Tsp = _round_up(Ts, SUBLANES)
    past_len = page_table.shape[1] * cache_k.shape[2]

    wd_b = ffn_w_down.astype(BF16)
    w1_b, a1_b = _pad_cols(decay_w1, LANES).astype(BF16), _pad_cols(aaa_a1, LANES).astype(BF16)
    g1_b = _pad_cols(gate_g1, LANES).astype(BF16)
    w2_b, a2_b = _pad_rows(decay_w2, LANES).astype(BF16), _pad_rows(aaa_a2, LANES).astype(BF16)
    g2_b = _pad_rows(gate_g2, LANES).astype(BF16)

    cos_p, sin_p = _rope_tables(jnp.arange(Tp, dtype=jnp.int32), head_dim)
    cos_s, sin_s = _rope_tables(past_len + jnp.arange(Tsp, dtype=jnp.int32), head_dim)

    x_s = jnp.pad(x_sample, ((0, 0), (0, Tsp - Ts), (0, 0)))
    groups = [dict(x=x_prompt.reshape(Bp * Tp, D), B=Bp, T=Tp, t_real=Tp),
              dict(x=x_s.reshape(Bs * Tsp, D), B=Bs, T=Tsp, t_real=Ts)]
    outs = [dict(k=[], v=[], wkv=[], shift=[]) for _ in groups]
    Ms = Bs * Tsp
    cos_st, sin_st = jnp.tile(cos_s, (Bs, 1)), jnp.tile(sin_s, (Bs, 1))
    bf16_rows = 2 * SUBLANES

    for layer in range(depth):
        pvec = jnp.stack([k_k[layer], k_a[layer], r_k[layer].reshape(-1), ln_x_w[layer], ln_x_b[layer]]
                         + [jnp.zeros((rw,), F32)] * (SUBLANES - 5))
        sprevs = [jnp.zeros((Bp, 1, D), F32), state_shift[layer][:, None, :]]
        pro = [_prologue(grp["x"].reshape(grp["B"], grp["T"], D), sprev, norm_mix[layer][None], mu_wag[layer],
                         w1_b[layer], a1_b[layer], g1_b[layer], w2_b[layer], a2_b[layer], g2_b[layer],
                         decay_w0[layer][None], aaa_a0[layer][None], grp["t_real"])
               for grp, sprev in zip(groups, sprevs)]
        xn = [p[0].reshape(grp["B"] * grp["T"], D) for p, grp in zip(pro, groups)]

        rope = dict(extra_p=(cos_p, sin_p), extra_s=(cos_st, sin_st), rows_per_seq=Tp, head_dim=head_dim)
        q = _ws_matmul([xn[0]], [xn[1]], [w_in], layer, 0, att_w, "rope", F32, **rope)
        k = _ws_matmul([xn[0]], [xn[1]], [w_in], layer, att_w, att_w, "rope", F32, **rope)
        v = _ws_matmul([xn[0]], [xn[1]], [w_in], layer, 2 * att_w, att_w, "plain", F32)
        shift_rows = _pad_rows(state_shift[layer], bf16_rows).astype(BF16)
        rkv = _ws_matmul([xn[0]], [jnp.concatenate([xn[1], shift_rows], axis=0)], [w_in], layer,
                         3 * att_w, 3 * rw, "plain", F32)
        y_rkv = [rkv[0], rkv[1][:Ms]]
        y_prev = [jnp.zeros((Bp, 1, 3 * rw), F32), rkv[1][Ms:Ms + Bs].reshape(Bs, 1, 3 * rw)]
        state0 = [None, state_wkv[layer]]

        o_att, o_rwkv = [], []
        for gi, (grp, out) in enumerate(zip(groups, outs)):
            B, T, tr = grp["B"], grp["T"], grp["t_real"]
            q3, k3, v3 = (z[gi].reshape(B, T, att_w) for z in (q, k, v))
            if gi == 0:
                o_att.append(_moba_prompt(q3, k3, v3, head_dim))
            else:
                o_att.append(_moba_sample_gather(q3, k3, v3, cache_k, cache_v, page_table, layer, tr))
            _, lw, a, g, shift = pro[gi]
            o, wkv_new = _wkv(y_rkv[gi].reshape(B, T, 3 * rw), y_prev[gi], lw, a, g, mu_rkv[layer], pvec,
                              state0[gi], tr)
            o_rwkv.append(o)
            out["k"].append(k3[:, :tr].reshape(B, tr, n_att_heads, head_dim))
            out["v"].append(v3[:, :tr].reshape(B, tr, n_att_heads, head_dim))
            out["wkv"].append(wkv_new)
            out["shift"].append(shift.reshape(B, D))

        rows = lambda z, grp: z.reshape(grp["B"] * grp["T"], -1)
        h = _ws_matmul([rows(o_att[0], groups[0]), rows(o_rwkv[0], groups[0])],
                       [rows(o_att[1], groups[1]), rows(o_rwkv[1], groups[1])],
                       [w_out], layer, 0, D, "residual", F32,
                       extra_p=(groups[0]["x"],), extra_s=(groups[1]["x"],))
        hn = [_rmsnorm(z, norm_ffn[layer][None], BF16) for z in h]
        act = _ws_matmul([hn[0]], [hn[1]], [ffn_w_gate, ffn_w_up], layer, 0, ffn_w_gate.shape[2], "swiglu", BF16)
        for gi, grp in enumerate(groups):
            grp["x"] = _matmul_residual([act[gi]], wd_b, layer, h[gi])

    ys = []
    for grp in groups:
        B, T = grp["B"], grp["T"]
        y = _rmsnorm(grp["x"], norm_final[None], F32).reshape(B, T, D)
        ys.append(y[:, :grp["t_real"]])
    po, so = outs
    return (ys[0], ys[1],
            jnp.stack(po["k"]), jnp.stack(po["v"]), jnp.stack(po["wkv"]), jnp.stack(po["shift"]),
            jnp.stack(so["k"]), jnp.stack(so["v"]), jnp.stack(so["wkv"]), jnp.stack(so["shift"]))
```

```python
import functools
import math

import jax
import jax.numpy as jnp
from jax import lax
from jax.experimental import pallas as pl
from jax.experimental.pallas import tpu as pltpu

MOBA_BLOCK = 256
MOBA_TOPK = 3
ROPE_THETA = 10000.0
RMS_EPS = 1e-6
GN_EPS = 64e-5
KK_EPS = 1e-12
DECAY_SCALE = math.exp(-0.5)

LANES = 128
SUBLANES = 8
WKV_CHUNK = 64
WKV_SUM_TERMS = 2
VMEM_LIMIT = 56 * 1024 * 1024

F32 = jnp.float32
BF16 = jnp.bfloat16
NEG_INF = float("-inf")


def _cparams(*sem):
    return pltpu.CompilerParams(dimension_semantics=sem, vmem_limit_bytes=VMEM_LIMIT)


def _round_up(n, m):
    return -(-n // m) * m


def _pick_tile(n, prefs):
    for p in prefs:
        if n % p == 0:
            return p
    return n


def _split3(x):
    h1 = x.astype(BF16)
    r1 = x - h1.astype(F32)
    h2 = r1.astype(BF16)
    h3 = (r1 - h2.astype(F32)).astype(BF16)
    return h1, h2, h3


def _dot(a, b):
    return jnp.dot(a, b, preferred_element_type=F32)


def _dot_nt(a, b):
    return lax.dot_general(a, b, (((1,), (1,)), ((), ())), preferred_element_type=F32)


def _dot_tn(a, b):
    return lax.dot_general(a, b, (((0,), (0,)), ((), ())), preferred_element_type=F32)


def _dot_exact_rhs(x, e, terms=3):
    m = x.shape[0]
    y = _dot(jnp.concatenate(_split3(x)[:terms], axis=0), e)
    return sum(y[i * m:(i + 1) * m] for i in range(1, terms)) + y[0:m]


def _dot_exact_lhs(e, x, terms=3):
    n = x.shape[1]
    y = _dot(e, jnp.concatenate(_split3(x)[:terms], axis=1))
    return sum(y[:, i * n:(i + 1) * n] for i in range(1, terms)) + y[:, 0:n]


def _prologue_kernel(x_ref, sprev_ref, g_ref, mu_ref, w1_ref, a1_ref, g1_ref, w2_ref, a2_ref,
                     g2_ref, w0_ref, a0_ref, xn_ref, lw_ref, a_ref, gate_ref, shift_ref,
                     carry_ref, *, last_tile, last_row):
    i = pl.program_id(1)
    x = x_ref[0]
    tm = x.shape[0]
    xn = x * lax.rsqrt(jnp.mean(x * x, axis=-1, keepdims=True) + RMS_EPS) * g_ref[...]

    @pl.when(i == 0)
    def _():
        carry_ref[...] = sprev_ref[0]

    row = lax.broadcasted_iota(jnp.int32, xn.shape, 0)
    prev = jnp.where(row == 0, carry_ref[...], pltpu.roll(xn, 1, axis=0))
    carry_ref[...] = xn[tm - 1:tm, :]
    dx = prev - xn
    xw = (xn + dx * mu_ref[0:1, :]).astype(BF16)
    xa = (xn + dx * mu_ref[1:2, :]).astype(BF16)
    xg = (xn + dx * mu_ref[2:3, :]).astype(BF16)

    hw = jnp.tanh(_dot(xw, w1_ref[...])).astype(BF16)
    zw = w0_ref[...] + _dot(hw, w2_ref[...])
    lw_ref[0] = -DECAY_SCALE * jax.nn.sigmoid(zw)
    ha = _dot(xa, a1_ref[...]).astype(BF16)
    a_ref[0] = jax.nn.sigmoid(a0_ref[...] + _dot(ha, a2_ref[...]))
    hg = jax.nn.sigmoid(_dot(xg, g1_ref[...])).astype(BF16)
    gate_ref[0] = _dot(hg, g2_ref[...])
    xn_ref[0] = xn.astype(BF16)

    @pl.when(i == last_tile)
    def _():
        shift_ref[0] = xn[last_row:last_row + 1, :]


def _prologue(x, sprev, g, mu, w1, a1, g1, w2, a2, g2, w0, a0, t_real):
    B, T, D = x.shape
    RW = w2.shape[1]
    tm = _pick_tile(T, (128, 64, 32, 16, 8))
    nt = T // tm
    full = lambda arr: pl.BlockSpec(arr.shape, lambda b, i: (0,) * arr.ndim)
    row_spec = lambda w: pl.BlockSpec((1, tm, w), lambda b, i: (b, i, 0))
    kern = functools.partial(_prologue_kernel, last_tile=(t_real - 1) // tm,
                             last_row=(t_real - 1) % tm)
    return pl.pallas_call(
        kern,
        grid=(B, nt),
        in_specs=[row_spec(D), pl.BlockSpec((1, 1, D), lambda b, i: (b, 0, 0)), full(g), full(mu),
                  full(w1), full(a1), full(g1), full(w2), full(a2), full(g2), full(w0), full(a0)],
        out_specs=[row_spec(D), row_spec(RW), row_spec(RW), row_spec(RW),
                   pl.BlockSpec((1, 1, D), lambda b, i: (b, 0, 0))],
        out_shape=[jax.ShapeDtypeStruct((B, T, D), BF16), jax.ShapeDtypeStruct((B, T, RW), F32),
                   jax.ShapeDtypeStruct((B, T, RW), F32), jax.ShapeDtypeStruct((B, T, RW), F32),
                   jax.ShapeDtypeStruct((B, 1, D), F32)],
        scratch_shapes=[pltpu.VMEM((1, D), F32)],
        compiler_params=_cparams("arbitrary", "arbitrary"),
    )(x, sprev, g, mu, w1, a1, g1, w2, a2, g2, w0, a0)


def _rmsnorm_kernel(x_ref, g_ref, o_ref):
    x = x_ref[...]
    y = x * lax.rsqrt(jnp.mean(x * x, axis=-1, keepdims=True) + RMS_EPS) * g_ref[...]
    o_ref[...] = y.astype(o_ref.dtype)


def _rmsnorm(x2d, g, out_dtype):
    M, D = x2d.shape
    tm = _pick_tile(M, (512, 256, 128, 64, 32, 16, 8))
    return pl.pallas_call(
        _rmsnorm_kernel,
        grid=(M // tm,),
        in_specs=[pl.BlockSpec((tm, D), lambda i: (i, 0)), pl.BlockSpec((1, D), lambda i: (0, 0))],
        out_specs=pl.BlockSpec((tm, D), lambda i: (i, 0)),
        out_shape=jax.ShapeDtypeStruct((M, D), out_dtype),
        compiler_params=_cparams("arbitrary"),
    )(x2d, g)


_WS_EXTRAS = {"plain": 0, "rope": 2, "swiglu": 0, "residual": 1}


def _ws_kernel(*refs, n_lhs, n_grp, mode, head_dim):
    it = iter(refs)
    lhs_p = [next(it) for _ in range(n_lhs)]
    lhs_s = [next(it) for _ in range(n_lhs)]
    w = [[next(it) for _ in range(n_lhs)] for _ in range(n_grp)]
    extra_p = [next(it) for _ in range(_WS_EXTRAS[mode])]
    extra_s = [next(it) for _ in range(_WS_EXTRAS[mode])]
    o_p, o_s = next(it), next(it)
    wb = [[next(it) for _ in range(n_lhs)] for _ in range(n_grp)]

    def emit(lhs, extra, o_ref):
        ys = []
        for g in range(n_grp):
            y = _dot(lhs[0][...], wb[g][0][...])
            for t in range(1, n_lhs):
                y = y + _dot(lhs[t][...], wb[g][t][...])
            ys.append(y)
        if mode == "plain":
            o_ref[...] = ys[0]
        elif mode == "residual":
            o_ref[...] = extra[0][...] + ys[0]
        elif mode == "swiglu":
            o_ref[...] = (ys[0] * jax.nn.sigmoid(ys[0]) * ys[1]).astype(o_ref.dtype)
        else:
            cos, sin = extra[0][...], extra[1][...]
            for h in range(ys[0].shape[1] // head_dim):
                yh = ys[0][:, h * head_dim:(h + 1) * head_dim]
                o_ref[:, h * head_dim:(h + 1) * head_dim] = (
                    yh * cos + pltpu.roll(yh, head_dim // 2, axis=1) * sin)

    @pl.when(pl.program_id(1) == 0)
    def _():
        for g in range(n_grp):
            for t in range(n_lhs):
                wb[g][t][...] = w[g][t][...].astype(BF16)
        emit(lhs_s, extra_s, o_s)

    emit(lhs_p, extra_p, o_p)


def _ws_matmul(lhs_p, lhs_s, weights, layer, col_off, n_cols, mode, out_dtype, extra_p=(), extra_s=(),
               rows_per_seq=None, head_dim=LANES):
    n_lhs, n_grp = len(lhs_p), len(weights)
    M, kblk = lhs_p[0].shape
    Ms = lhs_s[0].shape[0]
    tm = _pick_tile(math.gcd(M, rows_per_seq) if rows_per_seq else M, (1024, 512, 256, 128, 64, 32, 16, 8))
    out_bytes = jnp.dtype(out_dtype).itemsize
    for tn_pref in (512, 256, 128):
        tn = _pick_tile(math.gcd(n_cols, col_off) if col_off else n_cols,
                        tuple(p for p in (512, 256, 128) if p <= tn_pref))
        w_elems = n_grp * n_lhs * kblk * tn
        need = (w_elems * (2 * 4 + 2) + 2 * n_lhs * kblk * (tm + Ms) * 2
                + 2 * (tm + Ms) * tn * (out_bytes + (4 if mode == "residual" else 0)))
        if need <= VMEM_LIMIT * 7 // 8:
            break
    off = col_off // tn
    row_p = lambda width: pl.BlockSpec((tm, width), lambda j, i: (i, 0))
    all_s = lambda width: pl.BlockSpec((Ms, width), lambda j, i: (0, 0))
    in_specs = [row_p(kblk)] * n_lhs + [all_s(kblk)] * n_lhs
    in_specs += [pl.BlockSpec((None, kblk, tn), functools.partial(lambda j, i, t: (layer, t, j + off), t=t))
                 for _ in range(n_grp) for t in range(n_lhs)]
    args = list(lhs_p) + list(lhs_s) + [wt for wt in weights for _ in range(n_lhs)]
    if mode == "rope":
        per = rows_per_seq // tm
        in_specs += [pl.BlockSpec((tm, head_dim), lambda j, i: (i % per, 0))] * 2 + [all_s(head_dim)] * 2
    elif mode == "residual":
        in_specs += [pl.BlockSpec((tm, tn), lambda j, i: (i, j)), pl.BlockSpec((Ms, tn), lambda j, i: (0, j))]
    args += list(extra_p) + list(extra_s)
    return pl.pallas_call(
        functools.partial(_ws_kernel, n_lhs=n_lhs, n_grp=n_grp, mode=mode, head_dim=head_dim),
        grid=(n_cols // tn, M // tm),
        in_specs=in_specs,
        out_specs=[pl.BlockSpec((tm, tn), lambda j, i: (i, j)), pl.BlockSpec((Ms, tn), lambda j, i: (0, j))],
        out_shape=[jax.ShapeDtypeStruct((M, n_cols), out_dtype), jax.ShapeDtypeStruct((Ms, n_cols), out_dtype)],
        scratch_shapes=[pltpu.VMEM((kblk, tn), BF16) for _ in range(n_grp * n_lhs)],
        compiler_params=_cparams("arbitrary", "arbitrary"),
    )(*args)


def _mm_res_kernel(*refs, n_lhs):
    res_ref = refs[2 * n_lhs]
    o_ref = refs[2 * n_lhs + 1]
    acc = res_ref[...]
    for t in range(n_lhs):
        acc = acc + _dot(refs[t][...], refs[n_lhs + t][...])
    o_ref[...] = acc


def _matmul_residual(lhs_list, w_all, layer, res2d):
    M, N = res2d.shape
    n_lhs = len(lhs_list)
    ks = [l.shape[1] for l in lhs_list]
    kblk = ks[0]
    assert all(k == kblk for k in ks)
    for tm_pref, tn_pref in ((1024, 512), (512, 512), (512, 256), (256, 256), (256, 128)):
        tm = _pick_tile(M, tuple(p for p in (1024, 512, 256, 128, 64, 32, 16, 8) if p <= tm_pref))
        tn = _pick_tile(N, tuple(p for p in (512, 256, 128) if p <= tn_pref))
        if 2 * (n_lhs * kblk * (tm + tn) * 2 + 2 * tm * tn * 4) <= VMEM_LIMIT * 3 // 4:
            break
    in_specs = [pl.BlockSpec((tm, kblk), lambda i, j: (i, 0)) for _ in lhs_list]
    in_specs += [pl.BlockSpec((None, kblk, tn), functools.partial(lambda i, j, t: (layer, t, j), t=t))
                 for t in range(n_lhs)]
    in_specs += [pl.BlockSpec((tm, tn), lambda i, j: (i, j))]
    return pl.pallas_call(
        functools.partial(_mm_res_kernel, n_lhs=n_lhs),
        grid=(M // tm, N // tn),
        in_specs=in_specs,
        out_specs=pl.BlockSpec((tm, tn), lambda i, j: (i, j)),
        out_shape=jax.ShapeDtypeStruct((M, N), F32),
        compiler_params=_cparams("arbitrary", "arbitrary"),
    )(*lhs_list, *([w_all] * n_lhs), res2d)


def _rank_select(g, n_rows, row_of):
    row = lax.broadcasted_iota(jnp.int32, g.shape, 0)

    def body(m, cnt):
        gm = row_of(m)
        better = (gm > g) | ((gm == g) & (m < row))
        return cnt + better.astype(jnp.int32)

    cnt = lax.fori_loop(0, n_rows, body, jnp.zeros(g.shape, jnp.int32))
    return cnt < MOBA_TOPK


def _moba_prompt_kernel(q_ref, k_ref, v_ref, o_ref, means_ref, kb_ref, vt_ref, gate_ref, sel_ref,
                        *, nb, scale, n_heads, head_dim):
    qi = pl.program_id(2)
    blk = MOBA_BLOCK
    heads = range(n_heads)
    cols = [slice(h * head_dim, (h + 1) * head_dim) for h in heads]

    @pl.when(qi == 0)
    def _():
        means_ref[...] = jnp.zeros(means_ref.shape, F32)
        for h in heads:
            for n in range(nb):
                kn = k_ref[0, n * blk:(n + 1) * blk, cols[h]]
                means_ref[h, n:n + 1, :] = jnp.mean(kn, axis=0, keepdims=True)
                kb_ref[h, n] = kn.astype(BF16)
                vt_ref[h, n] = v_ref[0, n * blk:(n + 1) * blk, cols[h]].T.astype(BF16)

    q = [q_ref[0, :, cols[h]] for h in heads]
    qb = [z.astype(BF16) for z in q]

    for h in heads:
        mh, ml, _ = _split3(means_ref[h])
        qh, ql, _ = _split3(q[h])
        gate = _dot_nt(mh, qh) + _dot_nt(mh, ql) + _dot_nt(ml, qh)
        row = lax.broadcasted_iota(jnp.int32, gate.shape, 0)
        gate_ref[h] = jnp.where(row < qi, gate, NEG_INF)
    for h in heads:
        g = gate_ref[h]
        row = lax.broadcasted_iota(jnp.int32, g.shape, 0)
        sel = (row < qi) & _rank_select(g, nb, lambda m, h=h: gate_ref[h, pl.ds(m, 1), :])
        sel_f = sel.astype(F32)
        for n in range(nb):
            sel_ref[h, n] = sel_f[n:n + 1, :]

    s = [_dot_nt(kb_ref[h, qi], qb[h]) * scale for h in heads]
    kidx = lax.broadcasted_iota(jnp.int32, s[0].shape, 0)
    qidx = lax.broadcasted_iota(jnp.int32, s[0].shape, 1)
    s = [jnp.where(kidx <= qidx, z, NEG_INF) for z in s]
    m0 = [jnp.max(z, axis=0, keepdims=True) for z in s]
    p = [jnp.exp(s[h] - m0[h]) for h in heads]
    l0 = [jnp.sum(z, axis=0, keepdims=True) for z in p]
    acc0 = [_dot(vt_ref[h, qi], p[h].astype(BF16)) for h in heads]

    def body(n, carry):
        m, l, acc = carry
        s = [_dot_nt(kb_ref[h, n], qb[h]) * scale for h in heads]
        s = [jnp.where(sel_ref[h, n] > 0.0, s[h], NEG_INF) for h in heads]
        m_new = [jnp.maximum(m[h], jnp.max(s[h], axis=0, keepdims=True)) for h in heads]
        alpha = [jnp.exp(m[h] - m_new[h]) for h in heads]
        p = [jnp.exp(s[h] - m_new[h]) for h in heads]
        l = [l[h] * alpha[h] + jnp.sum(p[h], axis=0, keepdims=True) for h in heads]
        pv = [_dot(vt_ref[h, n], p[h].astype(BF16)) for h in heads]
        acc = [acc[h] * alpha[h] + pv[h] for h in heads]
        return m_new, l, acc

    _, l, acc = lax.fori_loop(0, qi, body, (m0, l0, acc0))
    for h in heads:
        o_ref[0, :, cols[h]] = (acc[h] / l[h]).T.astype(o_ref.dtype)


MOBA_HEADS_PER_STEP = 8
MOBA_ROUTE_BLOCKS_PER_STEP = 8
MOBA_GATHER_RING = 3


def _moba_prompt(q, k, v, head_dim):
    B, T, W = q.shape
    assert T % MOBA_BLOCK == 0 and head_dim == LANES
    H = W // head_dim
    hp = _pick_tile(H, (MOBA_HEADS_PER_STEP, 1))
    wp = hp * head_dim
    nb = T // MOBA_BLOCK
    nbp = _round_up(nb, SUBLANES)
    kern = functools.partial(_moba_prompt_kernel, nb=nb, scale=head_dim ** -0.5, n_heads=hp,
                             head_dim=head_dim)
    kv_spec = pl.BlockSpec((1, T, wp), lambda b, h, i: (b, 0, h))
    return pl.pallas_call(
        kern,
        grid=(B, H // hp, nb),
        in_specs=[pl.BlockSpec((1, MOBA_BLOCK, wp), lambda b, h, i: (b, i, h)), kv_spec, kv_spec],
        out_specs=pl.BlockSpec((1, MOBA_BLOCK, wp), lambda b, h, i: (b, i, h)),
        out_shape=jax.ShapeDtypeStruct((B, T, W), BF16),
        scratch_shapes=[pltpu.VMEM((hp, nbp, head_dim), F32),
                        pltpu.VMEM((hp, nb, MOBA_BLOCK, head_dim), BF16),
                        pltpu.VMEM((hp, nb, head_dim, MOBA_BLOCK), BF16),
                        pltpu.VMEM((hp, nbp, MOBA_BLOCK), F32),
                        pltpu.VMEM((hp, nbp, 1, MOBA_BLOCK), F32)],
        compiler_params=_cparams("arbitrary", "arbitrary", "arbitrary"),
    )(q, k, v)


def _moba_route_kernel(pt_ref, q_ref, *rest, ppb, nbk, n_heads):
    n_pages = len(rest) - 2
    k_refs = rest[:n_pages]
    ids_ref, gate_ref = rest[n_pages:]
    n = pl.program_id(1)
    dh = k_refs[0].shape[1]
    bps = n_pages // ppb
    for bb in range(bps):
        tot = jnp.zeros((n_heads, dh), F32)
        for r in k_refs[bb * ppb:(bb + 1) * ppb]:
            tot = tot + jnp.sum(r[...].reshape(-1, n_heads, dh), axis=0)
        mean = tot * (1.0 / MOBA_BLOCK)
        gate_ref[n * bps + bb] = jnp.sum(q_ref[0] * mean[None], axis=-1)

    @pl.when(n == nbk // bps - 1)
    def _():
        g = gate_ref[...]
        blk = lax.broadcasted_iota(jnp.int32, g.shape, 0)

        def body(m, cnt):
            gm = gate_ref[m][None]
            better = (gm > g) | ((gm == g) & (m < blk))
            return cnt + better.astype(jnp.int32)

        rank = lax.fori_loop(0, nbk, body, jnp.zeros(g.shape, jnp.int32))
        for j in range(MOBA_TOPK):
            ids_ref[0, j] = jnp.sum(jnp.where(rank == j, blk, 0), axis=0)


def _moba_gather_kernel(pt_ref, ids_ref, q_ref, ko_ref, vo_ref, ck_ref, cv_ref, o_ref,
                        kbuf, vbuf, sem, *, layer, ppb, n_q, n_heads, scale):
    b = pl.program_id(0)
    n_slots = n_q * MOBA_TOPK
    page = kbuf.shape[3]
    ids_q = ids_ref.shape[1] // (MOBA_TOPK * n_heads)

    def copies(h, slot):
        out = []
        for q in range(n_q):
            for j in range(MOBA_TOPK):
                blk = ids_ref[b, (j * ids_q + q) * n_heads + h]
                for p in range(ppb):
                    pid = pt_ref[b, blk * ppb + p]
                    s = q * MOBA_TOPK + j
                    out.append(pltpu.make_async_copy(ck_ref.at[layer, pid, :, h, :], kbuf.at[slot, s, p],
                                                     sem.at[0, slot]))
                    out.append(pltpu.make_async_copy(cv_ref.at[layer, pid, :, h, :], vbuf.at[slot, s, p],
                                                     sem.at[1, slot]))
        return out

    ring = kbuf.shape[0]
    for h0 in range(min(ring - 1, n_heads)):
        for c in copies(h0, h0):
            c.start()

    def head_body(h, carry):
        slot = lax.rem(h, ring)
        ahead = h + (ring - 1)

        @pl.when(ahead < n_heads)
        def _():
            for c in copies(ahead, lax.rem(ahead, ring)):
                c.start()

        qb = q_ref[0, h].astype(BF16)
        s_own = _dot_nt(qb, ko_ref[0, h].astype(BF16)) * scale
        kidx = lax.broadcasted_iota(jnp.int32, s_own.shape, 1)
        qidx = lax.broadcasted_iota(jnp.int32, s_own.shape, 0)
        s_own = jnp.where(kidx <= qidx, s_own, NEG_INF)

        for c in copies(h, slot):
            c.wait()
        kb = kbuf[slot].reshape(n_slots * ppb * page, -1).astype(BF16)
        vb = vbuf[slot].reshape(n_slots * ppb * page, -1).astype(BF16)
        s = _dot_nt(qb, kb) * scale
        col_q = lax.broadcasted_iota(jnp.int32, s.shape, 1) // (MOBA_TOPK * ppb * page)
        row_q = lax.broadcasted_iota(jnp.int32, s.shape, 0)
        s = jnp.where(col_q == row_q, s, NEG_INF)
        m = jnp.maximum(jnp.max(s_own, axis=1, keepdims=True), jnp.max(s, axis=1, keepdims=True))
        p_own = jnp.exp(s_own - m)
        p = jnp.exp(s - m)
        l = jnp.sum(p_own, axis=1, keepdims=True) + jnp.sum(p, axis=1, keepdims=True)
        acc = _dot(p_own.astype(BF16), vo_ref[0, h].astype(BF16)) + _dot(p.astype(BF16), vb)
        o_ref[0, h] = (acc / l).astype(o_ref.dtype)
        return carry

    lax.fori_loop(0, n_heads, head_body, 0)


def _moba_sample_gather(q, k_new, v_new, cache_k, cache_v, page_table, layer, n_q):
    B, Ts, W = q.shape
    page, H, dh = cache_k.shape[2:]
    n_pages = page_table.shape[1]
    assert MOBA_BLOCK % page == 0 and (n_pages * page) % MOBA_BLOCK == 0 and dh == LANES
    ppb = MOBA_BLOCK // page
    nbk = n_pages // ppb
    assert nbk >= MOBA_TOPK
    qr = _round_up(n_q, SUBLANES)
    qp = _round_up(n_q, 2 * SUBLANES)
    own = LANES
    assert n_q <= Ts <= own
    pad_rows = lambda z, r: jnp.pad(z, ((0, 0), (0, r - z.shape[1]), (0, 0)))
    heads_first = lambda z: z.reshape(B, -1, H, dh).transpose(0, 2, 1, 3)
    q_route = pad_rows(q[:, :n_q], qr).reshape(B, qr, H, dh)
    q_h = heads_first(pad_rows(q[:, :n_q], qp))
    ko_h, vo_h = heads_first(pad_rows(k_new, own)), heads_first(pad_rows(v_new, own))
    cache_k_flat = cache_k.reshape(cache_k.shape[:2] + (page * H, dh))

    bps = _pick_tile(nbk, (MOBA_ROUTE_BLOCKS_PER_STEP, 2, 1))
    pps = ppb * bps

    def page_spec(j):
        return pl.BlockSpec((None, None, page * H, dh),
                            lambda b, n, pt: (layer, pt[b, n * pps + j], 0, 0))

    ids = pl.pallas_call(
        functools.partial(_moba_route_kernel, ppb=ppb, nbk=nbk, n_heads=H),
        grid_spec=pltpu.PrefetchScalarGridSpec(
            num_scalar_prefetch=1,
            grid=(B, nbk // bps),
            in_specs=[pl.BlockSpec((1, qr, H, dh), lambda b, n, pt: (b, 0, 0, 0))]
                     + [page_spec(j) for j in range(pps)],
            out_specs=pl.BlockSpec((1, MOBA_TOPK, qr, H), lambda b, n, pt: (b, 0, 0, 0)),
            scratch_shapes=[pltpu.VMEM((nbk, qr, H), F32)]),
        out_shape=jax.ShapeDtypeStruct((B, MOBA_TOPK, qr, H), jnp.int32),
        compiler_params=_cparams("arbitrary", "arbitrary"),
    )(page_table, q_route, *([cache_k_flat] * pps))

    n_slots = n_q * MOBA_TOPK
    per_head = lambda rows: pl.BlockSpec((1, H, rows, dh), lambda b, pt, ids: (b, 0, 0, 0))
    o_h = pl.pallas_call(
        functools.partial(_moba_gather_kernel, layer=layer, ppb=ppb, n_q=n_q, n_heads=H, scale=dh ** -0.5),
        grid_spec=pltpu.PrefetchScalarGridSpec(
            num_scalar_prefetch=2,
            grid=(B,),
            in_specs=[per_head(qp), per_head(own), per_head(own),
                      pl.BlockSpec(memory_space=pl.ANY), pl.BlockSpec(memory_space=pl.ANY)],
            out_specs=per_head(qp),
            scratch_shapes=[pltpu.VMEM((MOBA_GATHER_RING, n_slots, ppb, page, dh), F32),
                            pltpu.VMEM((MOBA_GATHER_RING, n_slots, ppb, page, dh), F32),
                            pltpu.SemaphoreType.DMA((2, MOBA_GATHER_RING))]),
        out_shape=jax.ShapeDtypeStruct((B, H, qp, dh), BF16),
        compiler_params=_cparams("arbitrary"),
    )(page_table, ids.reshape(B, -1), q_h, ko_h, vo_h, cache_k, cache_v)
    o = o_h.transpose(0, 2, 1, 3).reshape(B, qp, W)
    return pad_rows(o[:, :n_q], Ts)


def _wkv_kernel(*refs, has_state, n_pairs, t_real, n_chunks, head):
    it = iter(refs)
    yr_ref, yk_ref, yv_ref = next(it), next(it), next(it)
    pr_ref, pk_ref, pv_ref = next(it), next(it), next(it)
    lw_ref, a_ref, g_ref = next(it), next(it), next(it)
    mu_ref, pvec_ref = next(it), next(it)
    tri_ref, seg_ref = next(it), next(it)
    s0_ref = next(it) if has_state else None
    o_ref, sout_ref = next(it), next(it)
    sv_ref, carry_ref = next(it), next(it)

    c = pl.program_id(2)
    C = yr_ref.shape[1]
    G = 2 * C
    lane_g = lax.broadcasted_iota(jnp.int32, (G, LANES), 1)
    row_g = lax.broadcasted_iota(jnp.int32, (G, LANES), 0)
    stack_mask = (row_g // C) == (lane_g // head)
    rr = lax.broadcasted_iota(jnp.int32, (G, G), 0)
    cc = lax.broadcasted_iota(jnp.int32, (G, G), 1)
    strict = ((rr // C) == (cc // C)) & ((cc % C) < (rr % C))
    incl_row = (lax.broadcasted_iota(jnp.int32, (C, 2 * G), 1) % C
                <= lax.broadcasted_iota(jnp.int32, (C, 2 * G), 0))
    pr_i = lax.broadcasted_iota(jnp.int32, (head, LANES), 0)
    pc_i = lax.broadcasted_iota(jnp.int32, (head, LANES), 1)
    place = [(pc_i == pr_i + h * head).astype(BF16) for h in range(2)]
    seg = seg_ref[...]
    tri = tri_ref[...]
    trow = lax.broadcasted_iota(jnp.int32, (C, LANES), 0)

    def stack(x):
        return jnp.where(stack_mask, jnp.concatenate([x, x], axis=0), 0.0)

    @pl.when(c == 0)
    def _():
        carry_ref[0:1, :] = pr_ref[0]
        carry_ref[1:2, :] = pk_ref[0]
        carry_ref[2:3, :] = pv_ref[0]
        for p in range(n_pairs):
            if has_state:
                blocks = [_dot_exact_rhs(s0_ref[0, 2 * p + h], place[h]) for h in range(2)]
                sv_ref[p] = jnp.concatenate(blocks, axis=0)
            else:
                sv_ref[p] = jnp.zeros((LANES, LANES), F32)

    pairs = range(n_pairs)
    lanes_of = [slice(p * LANES, (p + 1) * LANES) for p in pairs]

    def mixed(y_ref, idx, sl):
        y = y_ref[0, :, sl]
        prev = jnp.where(trow == 0, carry_ref[idx:idx + 1, sl], pltpu.roll(y, 1, axis=0))
        carry_ref[idx:idx + 1, sl] = y[C - 1:C, :]
        return y + (prev - y) * mu_ref[idx:idx + 1, sl]

    r = [mixed(yr_ref, 0, sl) for sl in lanes_of]
    k = [mixed(yk_ref, 1, sl) for sl in lanes_of]
    v = [mixed(yv_ref, 2, sl) for sl in lanes_of]
    a = [a_ref[0, :, sl] for sl in lanes_of]
    lw = [lw_ref[0, :, sl] for sl in lanes_of]
    kk = [k[p] * pvec_ref[0:1, lanes_of[p]] for p in pairs]
    k2 = [k[p] * (1.0 + (a[p] - 1.0) * pvec_ref[1:2, lanes_of[p]]) for p in pairs]
    sums = [_dot_exact_rhs(jnp.concatenate(
        [kk[p] * kk[p], r[p] * k2[p] * pvec_ref[2:3, lanes_of[p]]], axis=0), seg, WKV_SUM_TERMS)
        for p in pairs]
    kk = [kk[p] / jnp.maximum(jnp.sqrt(sums[p][0:C]), KK_EPS) for p in pairs]
    bonus = [sums[p][C:2 * C] for p in pairs]
    if t_real is not None:
        live = (c * C + trow) < t_real
        lw = [jnp.where(live, z, 0.0) for z in lw]
        kk = [jnp.where(live, z, 0.0) for z in kk]
        k2 = [jnp.where(live, z, 0.0) for z in k2]
        v = [jnp.where(live, z, 0.0) for z in v]
    bv = [kk[p] * a[p] for p in pairs]

    logp = [_dot_exact_lhs(tri, lw[p], WKV_SUM_TERMS) for p in pairs]
    logpc = [z[C - 1:C, :] for z in logp]
    lhs, rhs, vst, bk = [], [], [], []
    for p in pairs:
        inv_p = jnp.exp(-logp[p])
        tail = jnp.exp(logpc[p] - logp[p])
        a_t = -kk[p] * jnp.exp(logp[p] - lw[p])
        r_t = r[p] * jnp.exp(logp[p])
        lhs.append(jnp.concatenate([a_t, r_t], axis=0).astype(BF16))
        rhs.append(jnp.concatenate([stack(bv[p] * inv_p), stack(k2[p] * inv_p)], axis=0).astype(BF16))
        bk.append(jnp.concatenate([stack(bv[p] * tail), stack(k2[p] * tail)], axis=0).astype(BF16))
        vst.append(stack(v[p]).astype(BF16))

    quad = [_dot_nt(lhs[p], rhs[p]) for p in pairs]
    sv = [sv_ref[p] for p in pairs]
    ss = [_dot_nt(lhs[p], sv[p].astype(BF16)) for p in pairs]
    twice = lambda z: jnp.concatenate([z, z], axis=0)
    n_pow = [jnp.where(strict, twice(quad[p][0:C, 0:G]), 0.0).astype(BF16) for p in pairs]
    a_ak = [jnp.where(strict, twice(quad[p][0:C, G:2 * G]), 0.0).astype(BF16) for p in pairs]
    a_r = [jnp.where(incl_row, quad[p][C:2 * C, :], 0.0).astype(BF16) for p in pairs]

    u = [stack(ss[p][0:C]) + _dot(a_ak[p], vst[p]) for p in pairs]
    span = 1
    while 2 * span < C:
        both = [_dot(n_pow[p], jnp.concatenate([u[p].astype(BF16), n_pow[p]], axis=1)) for p in pairs]
        u = [u[p] + both[p][:, 0:LANES] for p in pairs]
        n_pow = [both[p][:, LANES:LANES + G].astype(BF16) for p in pairs]
        span *= 2
    u = [u[p] + _dot(n_pow[p], u[p].astype(BF16)) for p in pairs]
    uv = [jnp.concatenate([u[p].astype(BF16), vst[p]], axis=0) for p in pairs]
    o = [ss[p][C:2 * C] + _dot(a_r[p], uv[p]) for p in pairs]
    for p in pairs:
        sv_ref[p] = sv[p] * jnp.exp(logpc[p]) + _dot_tn(uv[p], bk[p])

    d = [o[p] - _dot_exact_rhs(o[p], seg, WKV_SUM_TERMS) * (1.0 / head) for p in pairs]
    var = [_dot_exact_rhs(d[p] * d[p], seg, WKV_SUM_TERMS) * (1.0 / head) for p in pairs]
    for p in pairs:
        sl = lanes_of[p]
        y = d[p] * lax.rsqrt(var[p] + GN_EPS) * pvec_ref[3:4, sl] + pvec_ref[4:5, sl]
        y = y + bonus[p] * v[p]
        o_ref[0, :, sl] = (y * g_ref[0, :, sl]).astype(o_ref.dtype)

    @pl.when(c == n_chunks - 1)
    def _():
        for p in range(n_pairs):
            for h in range(2):
                rows = sv_ref[p, h * head:(h + 1) * head, :]
                sout_ref[0, 2 * p + h] = _dot_nt(*_pair3(rows, place[h]))


def _pair3(x, e):
    h1, h2, h3 = _split3(x)
    return jnp.concatenate([h1, h2, h3], axis=1), jnp.concatenate([e, e, e], axis=1)


def _wkv(y_rkv, y_prev, lw, a, g, mu_rkv, pvec, state0, t_real):
    B, t_in, RW3 = y_rkv.shape
    RW = RW3 // 3
    head = LANES // 2
    NH = RW // head
    C = WKV_CHUNK
    T = _round_up(t_in, C)
    if T != t_in:
        pad_t = lambda z: jnp.pad(z, ((0, 0), (0, T - t_in), (0, 0)))
        y_rkv, lw, a, g = pad_t(y_rkv), pad_t(lw), pad_t(a), pad_t(g)
    n_chunks = T // C
    n_pairs = _pick_tile(RW // LANES, (16, 8, 4, 2, 1))
    lwd = n_pairs * LANES
    ncol = RW // lwd
    G = 2 * C
    tri = (jnp.arange(C)[:, None] >= jnp.arange(C)[None, :]).astype(BF16)
    seg = ((jnp.arange(LANES)[:, None] // head) == (jnp.arange(LANES)[None, :] // head)).astype(BF16)

    def cols(part):
        return pl.BlockSpec((1, C, lwd), lambda b, j, c: (b, c, j + part * ncol))

    def prev_cols(part):
        return pl.BlockSpec((1, 1, lwd), lambda b, j, c: (b, 0, j + part * ncol))

    tile = pl.BlockSpec((1, C, lwd), lambda b, j, c: (b, c, j))
    in_specs = [cols(0), cols(1), cols(2), prev_cols(0), prev_cols(1), prev_cols(2), tile, tile, tile,
                pl.BlockSpec((3, lwd), lambda b, j, c: (0, j)),
                pl.BlockSpec((SUBLANES, lwd), lambda b, j, c: (0, j)),
                pl.BlockSpec((C, C), lambda b, j, c: (0, 0)),
                pl.BlockSpec((LANES, LANES), lambda b, j, c: (0, 0))]
    args = [y_rkv, y_rkv, y_rkv, y_prev, y_prev, y_prev, lw, a, g, mu_rkv, pvec, tri, seg]
    state_spec = pl.BlockSpec((1, 2 * n_pairs, head, head), lambda b, j, c: (b, j, 0, 0))
    if state0 is not None:
        in_specs.append(state_spec)
        args.append(state0)
    kern = functools.partial(_wkv_kernel, has_state=state0 is not None, n_pairs=n_pairs,
                             t_real=None if t_real == T else t_real, n_chunks=n_chunks, head=head)
    o, state = pl.pallas_call(
        kern,
        grid=(B, ncol, n_chunks),
        in_specs=in_specs,
        out_specs=[tile, state_spec],
        out_shape=[jax.ShapeDtypeStruct((B, T, RW), BF16),
                   jax.ShapeDtypeStruct((B, NH, head, head), F32)],
        scratch_shapes=[pltpu.VMEM((n_pairs, LANES, LANES), F32), pltpu.VMEM((SUBLANES, lwd), F32)],
        compiler_params=_cparams("arbitrary", "arbitrary", "arbitrary"),
    )(*args)
    return o[:, :t_in], state


def _rope_tables(pos, head_dim):
    half = head_dim // 2
    inv_freq = ROPE_THETA ** (-jnp.arange(half, dtype=F32) / half)
    ang = pos.astype(F32)[:, None] * inv_freq[None, :]
    cos, sin = jnp.cos(ang), jnp.sin(ang)
    return jnp.concatenate([cos, cos], axis=1), jnp.concatenate([-sin, sin], axis=1)


def _pad_cols(w, mult):
    pad = _round_up(w.shape[-1], mult) - w.shape[-1]
    return jnp.pad(w, [(0, 0)] * (w.ndim - 1) + [(0, pad)]) if pad else w


def _pad_rows(w, mult):
    pad = _round_up(w.shape[-2], mult) - w.shape[-2]
    return jnp.pad(w, [(0, 0)] * (w.ndim - 2) + [(0, pad), (0, 0)]) if pad else w


def kernel(x_prompt, x_sample, cache_k, cache_v, state_wkv, state_shift, page_table, norm_mix, norm_ffn, norm_final, w_in, w_out, mu_rkv, mu_wag, decay_w0, decay_w1, decay_w2, aaa_a0, aaa_a1, aaa_a2, gate_g1, gate_g2, k_k, k_a, r_k, ln_x_w, ln_x_b, ffn_w_gate, ffn_w_up, ffn_w_down):
    depth = w_in.shape[0]
    D = x_prompt.shape[-1]
    n_att_heads, head_dim = cache_k.shape[3], cache_k.shape[4]
    att_w = n_att_heads * head_dim
    rw = mu_rkv.shape[-1]
    n_rwkv_heads, rwkv_head = r_k.shape[1], r_k.shape[2]
    assert rwkv_head * 2 == LANES and head_dim == LANES
    Bp, Tp, _ = x_prompt.shape
    Bs, Ts, _ = x_sample.shape
    Tsp = _round_up(Ts, SUBLANES)
    past_len = page_table.shape[1] * cache_k.shape[2]

    wd_b = ffn_w_down.astype(BF16)
    w1_b, a1_b = _pad_cols(decay_w1, LANES).astype(BF16), _pad_cols(aaa_a1, LANES).astype(BF16)
    g1_b = _pad_cols(gate_g1, LANES).astype(BF16)
    w2_b, a2_b = _pad_rows(decay_w2, LANES).astype(BF16), _pad_rows(aaa_a2, LANES).astype(BF16)
    g2_b = _pad_rows(gate_g2, LANES).astype(BF16)

    cos_p, sin_p = _rope_tables(jnp.arange(Tp, dtype=jnp.int32), head_dim)
    cos_s, sin_s = _rope_tables(past_len + jnp.arange(Tsp, dtype=jnp.int32), head_dim)

    x_s = jnp.pad(x_sample, ((0, 0), (0, Tsp - Ts), (0, 0)))
    groups = [dict(x=x_prompt.reshape(Bp * Tp, D), B=Bp, T=Tp, t_real=Tp),
              dict(x=x_s.reshape(Bs * Tsp, D), B=Bs, T=Tsp, t_real=Ts)]
    outs = [dict(k=[], v=[], wkv=[], shift=[]) for _ in groups]
    Ms = Bs * Tsp
    cos_st, sin_st = jnp.tile(cos_s, (Bs, 1)), jnp.tile(sin_s, (Bs, 1))
    bf16_rows = 2 * SUBLANES

    for layer in range(depth):
        pvec = jnp.stack([k_k[layer], k_a[layer], r_k[layer].reshape(-1), ln_x_w[layer], ln_x_b[layer]]
                         + [jnp.zeros((rw,), F32)] * (SUBLANES - 5))
        sprevs = [jnp.zeros((Bp, 1, D), F32), state_shift[layer][:, None, :]]
        pro = [_prologue(grp["x"].reshape(grp["B"], grp["T"], D), sprev, norm_mix[layer][None], mu_wag[layer],
                         w1_b[layer], a1_b[layer], g1_b[layer], w2_b[layer], a2_b[layer], g2_b[layer],
                         decay_w0[layer][None], aaa_a0[layer][None], grp["t_real"])
               for grp, sprev in zip(groups, sprevs)]
        xn = [p[0].reshape(grp["B"] * grp["T"], D) for p, grp in zip(pro, groups)]

        rope = dict(extra_p=(cos_p, sin_p), extra_s=(cos_st, sin_st), rows_per_seq=Tp, head_dim=head_dim)
        q = _ws_matmul([xn[0]], [xn[1]], [w_in], layer, 0, att_w, "rope", F32, **rope)
        k = _ws_matmul([xn[0]], [xn[1]], [w_in], layer, att_w, att_w, "rope", F32, **rope)
        v = _ws_matmul([xn[0]], [xn[1]], [w_in], layer, 2 * att_w, att_w, "plain", F32)
        shift_rows = _pad_rows(state_shift[layer], bf16_rows).astype(BF16)
        rkv = _ws_matmul([xn[0]], [jnp.concatenate([xn[1], shift_rows], axis=0)], [w_in], layer,
                         3 * att_w, 3 * rw, "plain", F32)
        y_rkv = [rkv[0], rkv[1][:Ms]]
        y_prev = [jnp.zeros((Bp, 1, 3 * rw), F32), rkv[1][Ms:Ms + Bs].reshape(Bs, 1, 3 * rw)]
        state0 = [None, state_wkv[layer]]

        o_att, o_rwkv = [], []
        for gi, (grp, out) in enumerate(zip(groups, outs)):
            B, T, tr = grp["B"], grp["T"], grp["t_real"]
            q3, k3, v3 = (z[gi].reshape(B, T, att_w) for z in (q, k, v))
            if gi == 0:
                o_att.append(_moba_prompt(q3, k3, v3, head_dim))
            else:
                o_att.append(_moba_sample_gather(q3, k3, v3, cache_k, cache_v, page_table, layer, tr))
            _, lw, a, g, shift = pro[gi]
            o, wkv_new = _wkv(y_rkv[gi].reshape(B, T, 3 * rw), y_prev[gi], lw, a, g, mu_rkv[layer], pvec,
                              state0[gi], tr)
            o_rwkv.append(o)
            out["k"].append(k3[:, :tr].reshape(B, tr, n_att_heads, head_dim))
            out["v"].append(v3[:, :tr].reshape(B, tr, n_att_heads, head_dim))
            out["wkv"].append(wkv_new)
            out["shift"].append(shift.reshape(B, D))

        rows = lambda z, grp: z.reshape(grp["B"] * grp["T"], -1)
        h = _ws_matmul([rows(o_att[0], groups[0]), rows(o_rwkv[0], groups[0])],
                       [rows(o_att[1], groups[1]), rows(o_rwkv[1], groups[1])],
                       [w_out], layer, 0, D, "residual", F32,
                       extra_p=(groups[0]["x"],), extra_s=(groups[1]["x"],))
        hn = [_rmsnorm(z, norm_ffn[layer][None], BF16) for z in h]
        act = _ws_matmul([hn[0]], [hn[1]], [ffn_w_gate, ffn_w_up], layer, 0, ffn_w_gate.shape[2], "swiglu", BF16)
        for gi, grp in enumerate(groups):
            grp["x"] = _matmul_residual([act[gi]], wd_b, layer, h[gi])

    ys = []
    for grp in groups:
        B, T = grp["B"], grp["T"]
        y = _rmsnorm(grp["x"], norm_final[None], F32).reshape(B, T, D)
        ys.append(y[:, :grp["t_real"]])
    po, so = outs
    return (ys[0], ys[1],
            jnp.stack(po["k"]), jnp.stack(po["v"]), jnp.stack(po["wkv"]), jnp.stack(po["shift"]),
            jnp.stack(so["k"]), jnp.stack(so["v"]), jnp.stack(so["wkv"]), jnp.stack(so["shift"]))
```

```python
import functools
import math

import jax
import jax.numpy as jnp
from jax import lax
from jax.experimental import pallas as pl
from jax.experimental.pallas import tpu as pltpu

MOBA_BLOCK = 256
MOBA_TOPK = 3
ROPE_THETA = 10000.0
RMS_EPS = 1e-6
GN_EPS = 64e-5
KK_EPS = 1e-12
DECAY_SCALE = math.exp(-0.5)

LANES = 128
SUBLANES = 8
WKV_CHUNK = 64
WKV_SUM_TERMS = 2
VMEM_LIMIT = 56 * 1024 * 1024

F32 = jnp.float32
BF16 = jnp.bfloat16
NEG_INF = float("-inf")


def _cparams(*sem):
    return pltpu.CompilerParams(dimension_semantics=sem, vmem_limit_bytes=VMEM_LIMIT)


def _round_up(n, m):
    return -(-n // m) * m


def _pick_tile(n, prefs):
    for p in prefs:
        if n % p == 0:
            return p
    return n


def _split3(x):
    h1 = x.astype(BF16)
    r1 = x - h1.astype(F32)
    h2 = r1.astype(BF16)
    h3 = (r1 - h2.astype(F32)).astype(BF16)
    return h1, h2, h3


def _dot(a, b):
    return jnp.dot(a, b, preferred_element_type=F32)


def _dot_nt(a, b):
    return lax.dot_general(a, b, (((1,), (1,)), ((), ())), preferred_element_type=F32)


def _dot_tn(a, b):
    return lax.dot_general(a, b, (((0,), (0,)), ((), ())), preferred_element_type=F32)


def _dot_exact_rhs(x, e, terms=3):
    m = x.shape[0]
    y = _dot(jnp.concatenate(_split3(x)[:terms], axis=0), e)
    return sum(y[i * m:(i + 1) * m] for i in range(1, terms)) + y[0:m]


def _dot_exact_lhs(e, x, terms=3):
    n = x.shape[1]
    y = _dot(e, jnp.concatenate(_split3(x)[:terms], axis=1))
    return sum(y[:, i * n:(i + 1) * n] for i in range(1, terms)) + y[:, 0:n]


def _prologue_kernel(x_ref, sprev_ref, g_ref, mu_ref, w1_ref, a1_ref, g1_ref, w2_ref, a2_ref,
                     g2_ref, w0_ref, a0_ref, xn_ref, lw_ref, a_ref, gate_ref, shift_ref,
                     carry_ref, *, last_tile, last_row):
    i = pl.program_id(1)
    x = x_ref[0]
    tm = x.shape[0]
    xn = x * lax.rsqrt(jnp.mean(x * x, axis=-1, keepdims=True) + RMS_EPS) * g_ref[...]

    @pl.when(i == 0)
    def _():
        carry_ref[...] = sprev_ref[0]

    row = lax.broadcasted_iota(jnp.int32, xn.shape, 0)
    prev = jnp.where(row == 0, carry_ref[...], pltpu.roll(xn, 1, axis=0))
    carry_ref[...] = xn[tm - 1:tm, :]
    dx = prev - xn
    xw = (xn + dx * mu_ref[0:1, :]).astype(BF16)
    xa = (xn + dx * mu_ref[1:2, :]).astype(BF16)
    xg = (xn + dx * mu_ref[2:3, :]).astype(BF16)

    hw = jnp.tanh(_dot(xw, w1_ref[...])).astype(BF16)
    zw = w0_ref[...] + _dot(hw, w2_ref[...])
    lw_ref[0] = -DECAY_SCALE * jax.nn.sigmoid(zw)
    ha = _dot(xa, a1_ref[...]).astype(BF16)
    a_ref[0] = jax.nn.sigmoid(a0_ref[...] + _dot(ha, a2_ref[...]))
    hg = jax.nn.sigmoid(_dot(xg, g1_ref[...])).astype(BF16)
    gate_ref[0] = _dot(hg, g2_ref[...])
    xn_ref[0] = xn.astype(BF16)

    @pl.when(i == last_tile)
    def _():
        shift_ref[0] = xn[last_row:last_row + 1, :]


def _prologue(x, sprev, g, mu, w1, a1, g1, w2, a2, g2, w0, a0, t_real):
    B, T, D = x.shape
    RW = w2.shape[1]
    tm = _pick_tile(T, (256, 128, 64, 32, 16, 8))
    nt = T // tm
    full = lambda arr: pl.BlockSpec(arr.shape, lambda b, i: (0,) * arr.ndim)
    row_spec = lambda w: pl.BlockSpec((1, tm, w), lambda b, i: (b, i, 0))
    kern = functools.partial(_prologue_kernel, last_tile=(t_real - 1) // tm,
                             last_row=(t_real - 1) % tm)
    return pl.pallas_call(
        kern,
        grid=(B, nt),
        in_specs=[row_spec(D), pl.BlockSpec((1, 1, D), lambda b, i: (b, 0, 0)), full(g), full(mu),
                  full(w1), full(a1), full(g1), full(w2), full(a2), full(g2), full(w0), full(a0)],
        out_specs=[row_spec(D), row_spec(RW), row_spec(RW), row_spec(RW),
                   pl.BlockSpec((1, 1, D), lambda b, i: (b, 0, 0))],
        out_shape=[jax.ShapeDtypeStruct((B, T, D), BF16), jax.ShapeDtypeStruct((B, T, RW), F32),
                   jax.ShapeDtypeStruct((B, T, RW), F32), jax.ShapeDtypeStruct((B, T, RW), F32),
                   jax.ShapeDtypeStruct((B, 1, D), F32)],
        scratch_shapes=[pltpu.VMEM((1, D), F32)],
        compiler_params=_cparams("arbitrary", "arbitrary"),
    )(x, sprev, g, mu, w1, a1, g1, w2, a2, g2, w0, a0)


def _rmsnorm_kernel(x_ref, g_ref, o_ref):
    x = x_ref[...]
    y = x * lax.rsqrt(jnp.mean(x * x, axis=-1, keepdims=True) + RMS_EPS) * g_ref[...]
    o_ref[...] = y.astype(o_ref.dtype)


def _rmsnorm(x2d, g, out_dtype):
    M, D = x2d.shape
    tm = _pick_tile(M, (512, 256, 128, 64, 32, 16, 8))
    return pl.pallas_call(
        _rmsnorm_kernel,
        grid=(M // tm,),
        in_specs=[pl.BlockSpec((tm, D), lambda i: (i, 0)), pl.BlockSpec((1, D), lambda i: (0, 0))],
        out_specs=pl.BlockSpec((tm, D), lambda i: (i, 0)),
        out_shape=jax.ShapeDtypeStruct((M, D), out_dtype),
        compiler_params=_cparams("arbitrary"),
    )(x2d, g)


_WS_EXTRAS = {"plain": 0, "rope": 2, "swiglu": 0, "residual": 1}


def _ws_kernel(*refs, n_lhs, n_grp, mode, head_dim):
    it = iter(refs)
    lhs_p = [next(it) for _ in range(n_lhs)]
    lhs_s = [next(it) for _ in range(n_lhs)]
    w = [[next(it) for _ in range(n_lhs)] for _ in range(n_grp)]
    extra_p = [next(it) for _ in range(_WS_EXTRAS[mode])]
    extra_s = [next(it) for _ in range(_WS_EXTRAS[mode])]
    o_p, o_s = next(it), next(it)
    wb = [[next(it) for _ in range(n_lhs)] for _ in range(n_grp)]

    def emit(lhs, extra, o_ref):
        ys = []
        for g in range(n_grp):
            y = _dot(lhs[0][...], wb[g][0][...])
            for t in range(1, n_lhs):
                y = y + _dot(lhs[t][...], wb[g][t][...])
            ys.append(y)
        if mode == "plain":
            o_ref[...] = ys[0]
        elif mode == "residual":
            o_ref[...] = extra[0][...] + ys[0]
        elif mode == "swiglu":
            o_ref[...] = (ys[0] * jax.nn.sigmoid(ys[0]) * ys[1]).astype(o_ref.dtype)
        else:
            cos, sin = extra[0][...], extra[1][...]
            for h in range(ys[0].shape[1] // head_dim):
                yh = ys[0][:, h * head_dim:(h + 1) * head_dim]
                o_ref[:, h * head_dim:(h + 1) * head_dim] = (
                    yh * cos + pltpu.roll(yh, head_dim // 2, axis=1) * sin)

    @pl.when(pl.program_id(1) == 0)
    def _():
        for g in range(n_grp):
            for t in range(n_lhs):
                wb[g][t][...] = w[g][t][...].astype(BF16)
        emit(lhs_s, extra_s, o_s)

    emit(lhs_p, extra_p, o_p)


def _ws_matmul(lhs_p, lhs_s, weights, layer, col_off, n_cols, mode, out_dtype, extra_p=(), extra_s=(),
               rows_per_seq=None, head_dim=LANES):
    n_lhs, n_grp = len(lhs_p), len(weights)
    M, kblk = lhs_p[0].shape
    Ms = lhs_s[0].shape[0]
    tm = _pick_tile(math.gcd(M, rows_per_seq) if rows_per_seq else M, (1024, 512, 256, 128, 64, 32, 16, 8))
    out_bytes = jnp.dtype(out_dtype).itemsize
    for tn_pref in (512, 256, 128):
        tn = _pick_tile(math.gcd(n_cols, col_off) if col_off else n_cols,
                        tuple(p for p in (512, 256, 128) if p <= tn_pref))
        w_elems = n_grp * n_lhs * kblk * tn
        need = (w_elems * (2 * 4 + 2) + 2 * n_lhs * kblk * (tm + Ms) * 2
                + 2 * (tm + Ms) * tn * (out_bytes + (4 if mode == "residual" else 0)))
        if need <= VMEM_LIMIT * 7 // 8:
            break
    off = col_off // tn
    row_p = lambda width: pl.BlockSpec((tm, width), lambda j, i: (i, 0))
    all_s = lambda width: pl.BlockSpec((Ms, width), lambda j, i: (0, 0))
    in_specs = [row_p(kblk)] * n_lhs + [all_s(kblk)] * n_lhs
    in_specs += [pl.BlockSpec((None, kblk, tn), functools.partial(lambda j, i, t: (layer, t, j + off), t=t))
                 for _ in range(n_grp) for t in range(n_lhs)]
    args = list(lhs_p) + list(lhs_s) + [wt for wt in weights for _ in range(n_lhs)]
    if mode == "rope":
        per = rows_per_seq // tm
        in_specs += [pl.BlockSpec((tm, head_dim), lambda j, i: (i % per, 0))] * 2 + [all_s(head_dim)] * 2
    elif mode == "residual":
        in_specs += [pl.BlockSpec((tm, tn), lambda j, i: (i, j)), pl.BlockSpec((Ms, tn), lambda j, i: (0, j))]
    args += list(extra_p) + list(extra_s)
    return pl.pallas_call(
        functools.partial(_ws_kernel, n_lhs=n_lhs, n_grp=n_grp, mode=mode, head_dim=head_dim),
        grid=(n_cols // tn, M // tm),
        in_specs=in_specs,
        out_specs=[pl.BlockSpec((tm, tn), lambda j, i: (i, j)), pl.BlockSpec((Ms, tn), lambda j, i: (0, j))],
        out_shape=[jax.ShapeDtypeStruct((M, n_cols), out_dtype), jax.ShapeDtypeStruct((Ms, n_cols), out_dtype)],
        scratch_shapes=[pltpu.VMEM((kblk, tn), BF16) for _ in range(n_grp * n_lhs)],
        compiler_params=_cparams("arbitrary", "arbitrary"),
    )(*args)


def _mm_res_kernel(*refs, n_lhs):
    res_ref = refs[2 * n_lhs]
    o_ref = refs[2 * n_lhs + 1]
    acc = res_ref[...]
    for t in range(n_lhs):
        acc = acc + _dot(refs[t][...], refs[n_lhs + t][...])
    o_ref[...] = acc


def _matmul_residual(lhs_list, w_all, layer, res2d):
    M, N = res2d.shape
    n_lhs = len(lhs_list)
    ks = [l.shape[1] for l in lhs_list]
    kblk = ks[0]
    assert all(k == kblk for k in ks)
    for tm_pref, tn_pref in ((1024, 512), (512, 512), (512, 256), (256, 256), (256, 128)):
        tm = _pick_tile(M, tuple(p for p in (1024, 512, 256, 128, 64, 32, 16, 8) if p <= tm_pref))
        tn = _pick_tile(N, tuple(p for p in (512, 256, 128) if p <= tn_pref))
        if 2 * (n_lhs * kblk * (tm + tn) * 2 + 2 * tm * tn * 4) <= VMEM_LIMIT * 3 // 4:
            break
    in_specs = [pl.BlockSpec((tm, kblk), lambda i, j: (i, 0)) for _ in lhs_list]
    in_specs += [pl.BlockSpec((None, kblk, tn), functools.partial(lambda i, j, t: (layer, t, j), t=t))
                 for t in range(n_lhs)]
    in_specs += [pl.BlockSpec((tm, tn), lambda i, j: (i, j))]
    return pl.pallas_call(
        functools.partial(_mm_res_kernel, n_lhs=n_lhs),
        grid=(M // tm, N // tn),
        in_specs=in_specs,
        out_specs=pl.BlockSpec((tm, tn), lambda i, j: (i, j)),
        out_shape=jax.ShapeDtypeStruct((M, N), F32),
        compiler_params=_cparams("arbitrary", "arbitrary"),
    )(*lhs_list, *([w_all] * n_lhs), res2d)


def _rank_select(g, n_rows, row_of):
    row = lax.broadcasted_iota(jnp.int32, g.shape, 0)

    def body(m, cnt):
        gm = row_of(m)
        better = (gm > g) | ((gm == g) & (m < row))
        return cnt + better.astype(jnp.int32)

    cnt = lax.fori_loop(0, n_rows, body, jnp.zeros(g.shape, jnp.int32))
    return cnt < MOBA_TOPK


def _moba_prompt_kernel(q_ref, k_ref, v_ref, o_ref, means_ref, kb_ref, vt_ref, gate_ref, sel_ref,
                        *, nb, scale, n_heads, head_dim):
    qi = pl.program_id(2)
    blk = MOBA_BLOCK
    heads = range(n_heads)
    cols = [slice(h * head_dim, (h + 1) * head_dim) for h in heads]

    @pl.when(qi == 0)
    def _():
        means_ref[...] = jnp.zeros(means_ref.shape, F32)
        for h in heads:
            for n in range(nb):
                kn = k_ref[0, n * blk:(n + 1) * blk, cols[h]]
                means_ref[h, n:n + 1, :] = jnp.mean(kn, axis=0, keepdims=True)
                kb_ref[h, n] = kn.astype(BF16)
                vt_ref[h, n] = v_ref[0, n * blk:(n + 1) * blk, cols[h]].T.astype(BF16)

    q = [q_ref[0, :, cols[h]] for h in heads]
    qb = [z.astype(BF16) for z in q]

    for h in heads:
        mh, ml, _ = _split3(means_ref[h])
        qh, ql, _ = _split3(q[h])
        gate = _dot_nt(mh, qh) + _dot_nt(mh, ql) + _dot_nt(ml, qh)
        row = lax.broadcasted_iota(jnp.int32, gate.shape, 0)
        gate_ref[h] = jnp.where(row < qi, gate, NEG_INF)
    for h in heads:
        g = gate_ref[h]
        row = lax.broadcasted_iota(jnp.int32, g.shape, 0)
        sel = (row < qi) & _rank_select(g, nb, lambda m, h=h: gate_ref[h, pl.ds(m, 1), :])
        sel_f = sel.astype(F32)
        for n in range(nb):
            sel_ref[h, n] = sel_f[n:n + 1, :]

    s = [_dot_nt(kb_ref[h, qi], qb[h]) * scale for h in heads]
    kidx = lax.broadcasted_iota(jnp.int32, s[0].shape, 0)
    qidx = lax.broadcasted_iota(jnp.int32, s[0].shape, 1)
    s = [jnp.where(kidx <= qidx, z, NEG_INF) for z in s]
    m0 = [jnp.max(z, axis=0, keepdims=True) for z in s]
    p = [jnp.exp(s[h] - m0[h]) for h in heads]
    l0 = [jnp.sum(z, axis=0, keepdims=True) for z in p]
    acc0 = [_dot(vt_ref[h, qi], p[h].astype(BF16)) for h in heads]

    def body(n, carry):
        m, l, acc = carry
        s = [_dot_nt(kb_ref[h, n], qb[h]) * scale for h in heads]
        s = [jnp.where(sel_ref[h, n] > 0.0, s[h], NEG_INF) for h in heads]
        m_new = [jnp.maximum(m[h], jnp.max(s[h], axis=0, keepdims=True)) for h in heads]
        alpha = [jnp.exp(m[h] - m_new[h]) for h in heads]
        p = [jnp.exp(s[h] - m_new[h]) for h in heads]
        l = [l[h] * alpha[h] + jnp.sum(p[h], axis=0, keepdims=True) for h in heads]
        pv = [_dot(vt_ref[h, n], p[h].astype(BF16)) for h in heads]
        acc = [acc[h] * alpha[h] + pv[h] for h in heads]
        return m_new, l, acc

    _, l, acc = lax.fori_loop(0, qi, body, (m0, l0, acc0))
    for h in heads:
        o_ref[0, :, cols[h]] = (acc[h] / l[h]).T.astype(o_ref.dtype)


MOBA_HEADS_PER_STEP = 8
MOBA_ROUTE_BLOCKS_PER_STEP = 8
MOBA_GATHER_RING = 3


def _moba_prompt(q, k, v, head_dim):
    B, T, W = q.shape
    assert T % MOBA_BLOCK == 0 and head_dim == LANES
    H = W // head_dim
    hp = _pick_tile(H, (MOBA_HEADS_PER_STEP, 1))
    wp = hp * head_dim
    nb = T // MOBA_BLOCK
    nbp = _round_up(nb, SUBLANES)
    kern = functools.partial(_moba_prompt_kernel, nb=nb, scale=head_dim ** -0.5, n_heads=hp,
                             head_dim=head_dim)
    kv_spec = pl.BlockSpec((1, T, wp), lambda b, h, i: (b, 0, h))
    return pl.pallas_call(
        kern,
        grid=(B, H // hp, nb),
        in_specs=[pl.BlockSpec((1, MOBA_BLOCK, wp), lambda b, h, i: (b, i, h)), kv_spec, kv_spec],
        out_specs=pl.BlockSpec((1, MOBA_BLOCK, wp), lambda b, h, i: (b, i, h)),
        out_shape=jax.ShapeDtypeStruct((B, T, W), BF16),
        scratch_shapes=[pltpu.VMEM((hp, nbp, head_dim), F32),
                        pltpu.VMEM((hp, nb, MOBA_BLOCK, head_dim), BF16),
                        pltpu.VMEM((hp, nb, head_dim, MOBA_BLOCK), BF16),
                        pltpu.VMEM((hp, nbp, MOBA_BLOCK), F32),
                        pltpu.VMEM((hp, nbp, 1, MOBA_BLOCK), F32)],
        compiler_params=_cparams("arbitrary", "arbitrary", "arbitrary"),
    )(q, k, v)


def _moba_route_kernel(pt_ref, q_ref, *rest, ppb, nbk, n_heads):
    n_pages = len(rest) - 2
    k_refs = rest[:n_pages]
    ids_ref, gate_ref = rest[n_pages:]
    n = pl.program_id(1)
    dh = k_refs[0].shape[1]
    bps = n_pages // ppb
    for bb in range(bps):
        tot = jnp.zeros((n_heads, dh), F32)
        for r in k_refs[bb * ppb:(bb + 1) * ppb]:
            tot = tot + jnp.sum(r[...].reshape(-1, n_heads, dh), axis=0)
        mean = tot * (1.0 / MOBA_BLOCK)
        gate_ref[n * bps + bb] = jnp.sum(q_ref[0] * mean[None], axis=-1)

    @pl.when(n == nbk // bps - 1)
    def _():
        g = gate_ref[...]
        blk = lax.broadcasted_iota(jnp.int32, g.shape, 0)

        def body(m, cnt):
            gm = gate_ref[m][None]
            better = (gm > g) | ((gm == g) & (m < blk))
            return cnt + better.astype(jnp.int32)

        rank = lax.fori_loop(0, nbk, body, jnp.zeros(g.shape, jnp.int32))
        for j in range(MOBA_TOPK):
            ids_ref[0, j] = jnp.sum(jnp.where(rank == j, blk, 0), axis=0)


def _moba_gather_kernel(pt_ref, ids_ref, q_ref, ko_ref, vo_ref, ck_ref, cv_ref, o_ref,
                        kbuf, vbuf, sem, *, layer, ppb, n_q, n_heads, scale):
    b = pl.program_id(0)
    n_slots = n_q * MOBA_TOPK
    page = kbuf.shape[3]
    ids_q = ids_ref.shape[1] // (MOBA_TOPK * n_heads)

    def copies(h, slot):
        out = []
        for q in range(n_q):
            for j in range(MOBA_TOPK):
                blk = ids_ref[b, (j * ids_q + q) * n_heads + h]
                for p in range(ppb):
                    pid = pt_ref[b, blk * ppb + p]
                    s = q * MOBA_TOPK + j
                    out.append(pltpu.make_async_copy(ck_ref.at[layer, pid, :, h, :], kbuf.at[slot, s, p],
                                                     sem.at[0, slot]))
                    out.append(pltpu.make_async_copy(cv_ref.at[layer, pid, :, h, :], vbuf.at[slot, s, p],
                                                     sem.at[1, slot]))
        return out

    ring = kbuf.shape[0]
    for h0 in range(min(ring - 1, n_heads)):
        for c in copies(h0, h0):
            c.start()

    def head_body(h, carry):
        slot = lax.rem(h, ring)
        ahead = h + (ring - 1)

        @pl.when(ahead < n_heads)
        def _():
            for c in copies(ahead, lax.rem(ahead, ring)):
                c.start()

        qb = q_ref[0, h].astype(BF16)
        s_own = _dot_nt(qb, ko_ref[0, h].astype(BF16)) * scale
        kidx = lax.broadcasted_iota(jnp.int32, s_own.shape, 1)
        qidx = lax.broadcasted_iota(jnp.int32, s_own.shape, 0)
        s_own = jnp.where(kidx <= qidx, s_own, NEG_INF)

        for c in copies(h, slot):
            c.wait()
        kb = kbuf[slot].reshape(n_slots * ppb * page, -1).astype(BF16)
        vb = vbuf[slot].reshape(n_slots * ppb * page, -1).astype(BF16)
        s = _dot_nt(qb, kb) * scale
        col_q = lax.broadcasted_iota(jnp.int32, s.shape, 1) // (MOBA_TOPK * ppb * page)
        row_q = lax.broadcasted_iota(jnp.int32, s.shape, 0)
        s = jnp.where(col_q == row_q, s, NEG_INF)
        m = jnp.maximum(jnp.max(s_own, axis=1, keepdims=True), jnp.max(s, axis=1, keepdims=True))
        p_own = jnp.exp(s_own - m)
        p = jnp.exp(s - m)
        l = jnp.sum(p_own, axis=1, keepdims=True) + jnp.sum(p, axis=1, keepdims=True)
        acc = _dot(p_own.astype(BF16), vo_ref[0, h].astype(BF16)) + _dot(p.astype(BF16), vb)
        o_ref[0, h] = (acc / l).astype(o_ref.dtype)
        return carry

    lax.fori_loop(0, n_heads, head_body, 0)


def _moba_sample_gather(q, k_new, v_new, cache_k, cache_v, page_table, layer, n_q):
    B, Ts, W = q.shape
    page, H, dh = cache_k.shape[2:]
    n_pages = page_table.shape[1]
    assert MOBA_BLOCK % page == 0 and (n_pages * page) % MOBA_BLOCK == 0 and dh == LANES
    ppb = MOBA_BLOCK // page
    nbk = n_pages // ppb
    assert nbk >= MOBA_TOPK
    qr = _round_up(n_q, SUBLANES)
    qp = _round_up(n_q, 2 * SUBLANES)
    own = LANES
    assert n_q <= Ts <= own
    pad_rows = lambda z, r: jnp.pad(z, ((0, 0), (0, r - z.shape[1]), (0, 0)))
    heads_first = lambda z: z.reshape(B, -1, H, dh).transpose(0, 2, 1, 3)
    q_route = pad_rows(q[:, :n_q], qr).reshape(B, qr, H, dh)
    q_h = heads_first(pad_rows(q[:, :n_q], qp))
    ko_h, vo_h = heads_first(pad_rows(k_new, own)), heads_first(pad_rows(v_new, own))
    cache_k_flat = cache_k.reshape(cache_k.shape[:2] + (page * H, dh))

    bps = _pick_tile(nbk, (MOBA_ROUTE_BLOCKS_PER_STEP, 2, 1))
    pps = ppb * bps

    def page_spec(j):
        return pl.BlockSpec((None, None, page * H, dh),
                            lambda b, n, pt: (layer, pt[b, n * pps + j], 0, 0))

    ids = pl.pallas_call(
        functools.partial(_moba_route_kernel, ppb=ppb, nbk=nbk, n_heads=H),
        grid_spec=pltpu.PrefetchScalarGridSpec(
            num_scalar_prefetch=1,
            grid=(B, nbk // bps),
            in_specs=[pl.BlockSpec((1, qr, H, dh), lambda b, n, pt: (b, 0, 0, 0))]
                     + [page_spec(j) for j in range(pps)],
            out_specs=pl.BlockSpec((1, MOBA_TOPK, qr, H), lambda b, n, pt: (b, 0, 0, 0)),
            scratch_shapes=[pltpu.VMEM((nbk, qr, H), F32)]),
        out_shape=jax.ShapeDtypeStruct((B, MOBA_TOPK, qr, H), jnp.int32),
        compiler_params=_cparams("arbitrary", "arbitrary"),
    )(page_table, q_route, *([cache_k_flat] * pps))

    n_slots = n_q * MOBA_TOPK
    per_head = lambda rows: pl.BlockSpec((1, H, rows, dh), lambda b, pt, ids: (b, 0, 0, 0))
    o_h = pl.pallas_call(
        functools.partial(_moba_gather_kernel, layer=layer, ppb=ppb, n_q=n_q, n_heads=H, scale=dh ** -0.5),
        grid_spec=pltpu.PrefetchScalarGridSpec(
            num_scalar_prefetch=2,
            grid=(B,),
            in_specs=[per_head(qp), per_head(own), per_head(own),
                      pl.BlockSpec(memory_space=pl.ANY), pl.BlockSpec(memory_space=pl.ANY)],
            out_specs=per_head(qp),
            scratch_shapes=[pltpu.VMEM((MOBA_GATHER_RING, n_slots, ppb, page, dh), F32),
                            pltpu.VMEM((MOBA_GATHER_RING, n_slots, ppb, page, dh), F32),
                            pltpu.SemaphoreType.DMA((2, MOBA_GATHER_RING))]),
        out_shape=jax.ShapeDtypeStruct((B, H, qp, dh), BF16),
        compiler_params=_cparams("arbitrary"),
    )(page_table, ids.reshape(B, -1), q_h, ko_h, vo_h, cache_k, cache_v)
    o = o_h.transpose(0, 2, 1, 3).reshape(B, qp, W)
    return pad_rows(o[:, :n_q], Ts)


def _wkv_kernel(*refs, has_state, n_pairs, t_real, n_chunks, head):
    it = iter(refs)
    yr_ref, yk_ref, yv_ref = next(it), next(it), next(it)
    pr_ref, pk_ref, pv_ref = next(it), next(it), next(it)
    lw_ref, a_ref, g_ref = next(it), next(it), next(it)
    mu_ref, pvec_ref = next(it), next(it)
    tri_ref, seg_ref = next(it), next(it)
    s0_ref = next(it) if has_state else None
    o_ref, sout_ref = next(it), next(it)
    sv_ref, carry_ref = next(it), next(it)

    c = pl.program_id(2)
    C = yr_ref.shape[1]
    G = 2 * C
    lane_g = lax.broadcasted_iota(jnp.int32, (G, LANES), 1)
    row_g = lax.broadcasted_iota(jnp.int32, (G, LANES), 0)
    stack_mask = (row_g // C) == (lane_g // head)
    rr = lax.broadcasted_iota(jnp.int32, (G, G), 0)
    cc = lax.broadcasted_iota(jnp.int32, (G, G), 1)
    strict = ((rr // C) == (cc // C)) & ((cc % C) < (rr % C))
    incl_row = (lax.broadcasted_iota(jnp.int32, (C, 2 * G), 1) % C
                <= lax.broadcasted_iota(jnp.int32, (C, 2 * G), 0))
    pr_i = lax.broadcasted_iota(jnp.int32, (head, LANES), 0)
    pc_i = lax.broadcasted_iota(jnp.int32, (head, LANES), 1)
    place = [(pc_i == pr_i + h * head).astype(BF16) for h in range(2)]
    seg = seg_ref[...]
    tri = tri_ref[...]
    trow = lax.broadcasted_iota(jnp.int32, (C, LANES), 0)

    def stack(x):
        return jnp.where(stack_mask, jnp.concatenate([x, x], axis=0), 0.0)

    @pl.when(c == 0)
    def _():
        carry_ref[0:1, :] = pr_ref[0]
        carry_ref[1:2, :] = pk_ref[0]
        carry_ref[2:3, :] = pv_ref[0]
        for p in range(n_pairs):
            if has_state:
                blocks = [_dot_exact_rhs(s0_ref[0, 2 * p + h], place[h]) for h in range(2)]
                sv_ref[p] = jnp.concatenate(blocks, axis=0)
            else:
                sv_ref[p] = jnp.zeros((LANES, LANES), F32)

    pairs = range(n_pairs)
    lanes_of = [slice(p * LANES, (p + 1) * LANES) for p in pairs]

    def mixed(y_ref, idx, sl):
        y = y_ref[0, :, sl]
        prev = jnp.where(trow == 0, carry_ref[idx:idx + 1, sl], pltpu.roll(y, 1, axis=0))
        carry_ref[idx:idx + 1, sl] = y[C - 1:C, :]
        return y + (prev - y) * mu_ref[idx:idx + 1, sl]

    r = [mixed(yr_ref, 0, sl) for sl in lanes_of]
    k = [mixed(yk_ref, 1, sl) for sl in lanes_of]
    v = [mixed(yv_ref, 2, sl) for sl in lanes_of]
    a = [a_ref[0, :, sl] for sl in lanes_of]
    lw = [lw_ref[0, :, sl] for sl in lanes_of]
    kk = [k[p] * pvec_ref[0:1, lanes_of[p]] for p in pairs]
    k2 = [k[p] * (1.0 + (a[p] - 1.0) * pvec_ref[1:2, lanes_of[p]]) for p in pairs]
    sums = [_dot_exact_rhs(jnp.concatenate(
        [kk[p] * kk[p], r[p] * k2[p] * pvec_ref[2:3, lanes_of[p]]], axis=0), seg, WKV_SUM_TERMS)
        for p in pairs]
    kk = [kk[p] / jnp.maximum(jnp.sqrt(sums[p][0:C]), KK_EPS) for p in pairs]
    bonus = [sums[p][C:2 * C] for p in pairs]
    if t_real is not None:
        live = (c * C + trow) < t_real
        lw = [jnp.where(live, z, 0.0) for z in lw]
        kk = [jnp.where(live, z, 0.0) for z in kk]
        k2 = [jnp.where(live, z, 0.0) for z in k2]
        v = [jnp.where(live, z, 0.0) for z in v]
    bv = [kk[p] * a[p] for p in pairs]

    logp = [_dot_exact_lhs(tri, lw[p], WKV_SUM_TERMS) for p in pairs]
    logpc = [z[C - 1:C, :] for z in logp]
    lhs, rhs, vst, bk = [], [], [], []
    for p in pairs:
        inv_p = jnp.exp(-logp[p])
        tail = jnp.exp(logpc[p] - logp[p])
        a_t = -kk[p] * jnp.exp(logp[p] - lw[p])
        r_t = r[p] * jnp.exp(logp[p])
        lhs.append(jnp.concatenate([a_t, r_t], axis=0).astype(BF16))
        rhs.append(jnp.concatenate([stack(bv[p] * inv_p), stack(k2[p] * inv_p)], axis=0).astype(BF16))
        bk.append(jnp.concatenate([stack(bv[p] * tail), stack(k2[p] * tail)], axis=0).astype(BF16))
        vst.append(stack(v[p]).astype(BF16))

    quad = [_dot_nt(lhs[p], rhs[p]) for p in pairs]
    sv = [sv_ref[p] for p in pairs]
    ss = [_dot_nt(lhs[p], sv[p].astype(BF16)) for p in pairs]
    twice = lambda z: jnp.concatenate([z, z], axis=0)
    n_pow = [jnp.where(strict, twice(quad[p][0:C, 0:G]), 0.0).astype(BF16) for p in pairs]
    a_ak = [jnp.where(strict, twice(quad[p][0:C, G:2 * G]), 0.0).astype(BF16) for p in pairs]
    a_r = [jnp.where(incl_row, quad[p][C:2 * C, :], 0.0).astype(BF16) for p in pairs]

    u = [stack(ss[p][0:C]) + _dot(a_ak[p], vst[p]) for p in pairs]
    span = 1
    while 2 * span < C:
        both = [_dot(n_pow[p], jnp.concatenate([u[p].astype(BF16), n_pow[p]], axis=1)) for p in pairs]
        u = [u[p] + both[p][:, 0:LANES] for p in pairs]
        n_pow = [both[p][:, LANES:LANES + G].astype(BF16) for p in pairs]
        span *= 2
    u = [u[p] + _dot(n_pow[p], u[p].astype(BF16)) for p in pairs]
    uv = [jnp.concatenate([u[p].astype(BF16), vst[p]], axis=0) for p in pairs]
    o = [ss[p][C:2 * C] + _dot(a_r[p], uv[p]) for p in pairs]
    for p in pairs:
        sv_ref[p] = sv[p] * jnp.exp(logpc[p]) + _dot_tn(uv[p], bk[p])

    d = [o[p] - _dot_exact_rhs(o[p], seg, WKV_SUM_TERMS) * (1.0 / head) for p in pairs]
    var = [_dot_exact_rhs(d[p] * d[p], seg, WKV_SUM_TERMS) * (1.0 / head) for p in pairs]
    for p in pairs:
        sl = lanes_of[p]
        y = d[p] * lax.rsqrt(var[p] + GN_EPS) * pvec_ref[3:4, sl] + pvec_ref[4:5, sl]
        y = y + bonus[p] * v[p]
        o_ref[0, :, sl] = (y * g_ref[0, :, sl]).astype(o_ref.dtype)

    @pl.when(c == n_chunks - 1)
    def _():
        for p in range(n_pairs):
            for h in range(2):
                rows = sv_ref[p, h * head:(h + 1) * head, :]
                sout_ref[0, 2 * p + h] = _dot_nt(*_pair3(rows, place[h]))


def _pair3(x, e):
    h1, h2, h3 = _split3(x)
    return jnp.concatenate([h1, h2, h3], axis=1), jnp.concatenate([e, e, e], axis=1)


def _wkv(y_rkv, y_prev, lw, a, g, mu_rkv, pvec, state0, t_real):
    B, t_in, RW3 = y_rkv.shape
    RW = RW3 // 3
    head = LANES // 2
    NH = RW // head
    C = WKV_CHUNK
    T = _round_up(t_in, C)
    if T != t_in:
        pad_t = lambda z: jnp.pad(z, ((0, 0), (0, T - t_in), (0, 0)))
        y_rkv, lw, a, g = pad_t(y_rkv), pad_t(lw), pad_t(a), pad_t(g)
    n_chunks = T // C
    n_pairs = _pick_tile(RW // LANES, (16, 8, 4, 2, 1))
    lwd = n_pairs * LANES
    ncol = RW // lwd
    tri = (jnp.arange(C)[:, None] >= jnp.arange(C)[None, :]).astype(BF16)
    seg = ((jnp.arange(LANES)[:, None] // head) == (jnp.arange(LANES)[None, :] // head)).astype(BF16)

    def cols(part):
        return pl.BlockSpec((1, C, lwd), lambda b, j, c: (b, c, j + part * ncol))

    def prev_cols(part):
        return pl.BlockSpec((1, 1, lwd), lambda b, j, c: (b, 0, j + part * ncol))

    tile = pl.BlockSpec((1, C, lwd), lambda b, j, c: (b, c, j))
    in_specs = [cols(0), cols(1), cols(2), prev_cols(0), prev_cols(1), prev_cols(2), tile, tile, tile,
                pl.BlockSpec((3, lwd), lambda b, j, c: (0, j)),
                pl.BlockSpec((SUBLANES, lwd), lambda b, j, c: (0, j)),
                pl.BlockSpec((C, C), lambda b, j, c: (0, 0)),
                pl.BlockSpec((LANES, LANES), lambda b, j, c: (0, 0))]
    args = [y_rkv, y_rkv, y_rkv, y_prev, y_prev, y_prev, lw, a, g, mu_rkv, pvec, tri, seg]
    state_spec = pl.BlockSpec((1, 2 * n_pairs, head, head), lambda b, j, c: (b, j, 0, 0))
    if state0 is not None:
        in_specs.append(state_spec)
        args.append(state0)
    kern = functools.partial(_wkv_kernel, has_state=state0 is not None, n_pairs=n_pairs,
                             t_real=None if t_real == T else t_real, n_chunks=n_chunks, head=head)
    o, state = pl.pallas_call(
        kern,
        grid=(B, ncol, n_chunks),
        in_specs=in_specs,
        out_specs=[tile, state_spec],
        out_shape=[jax.ShapeDtypeStruct((B, T, RW), BF16),
                   jax.ShapeDtypeStruct((B, NH, head, head), F32)],
        scratch_shapes=[pltpu.VMEM((n_pairs, LANES, LANES), F32), pltpu.VMEM((SUBLANES, lwd), F32)],
        compiler_params=_cparams("arbitrary", "arbitrary", "arbitrary"),
    )(*args)
    return o[:, :t_in], state


def _rope_tables(pos, head_dim):
    half = head_dim // 2
    inv_freq = ROPE_THETA ** (-jnp.arange(half, dtype=F32) / half)
    ang = pos.astype(F32)[:, None] * inv_freq[None, :]
    cos, sin = jnp.cos(ang), jnp.sin(ang)
    return jnp.concatenate([cos, cos], axis=1), jnp.concatenate([-sin, sin], axis=1)


def _pad_cols(w, mult):
    pad = _round_up(w.shape[-1], mult) - w.shape[-1]
    return jnp.pad(w, [(0, 0)] * (w.ndim - 1) + [(0, pad)]) if pad else w


def _pad_rows(w, mult):
    pad = _round_up(w.shape[-2], mult) - w.shape[-2]
    return jnp.pad(w, [(0, 0)] * (w.ndim - 2) + [(0, pad), (0, 0)]) if pad else w


def kernel(x_prompt, x_sample, cache_k, cache_v, state_wkv, state_shift, page_table, norm_mix, norm_ffn, norm_final, w_in, w_out, mu_rkv, mu_wag, decay_w0, decay_w1, decay_w2, aaa_a0, aaa_a1, aaa_a2, gate_g1, gate_g2, k_k, k_a, r_k, ln_x_w, ln_x_b, ffn_w_gate, ffn_w_up, ffn_w_down):
    depth = w_in.shape[0]
    D = x_prompt.shape[-1]
    n_att_heads, head_dim = cache_k.shape[3], cache_k.shape[4]
    att_w = n_att_heads * head_dim
    rw = mu_rkv.shape[-1]
    n_rwkv_heads, rwkv_head = r_k.shape[1], r_k.shape[2]
    assert rwkv_head * 2 == LANES and head_dim == LANES
    Bp, Tp, _ = x_prompt.shape
    Bs, Ts, _ = x_sample.shape
    Tsp = _round_up(Ts, SUBLANES)
    past_len = page_table.shape[1] * cache_k.shape[2]

    wd_b = ffn_w_down.astype(BF16)
    w1_b, a1_b = _pad_cols(decay_w1, LANES).astype(BF16), _pad_cols(aaa_a1, LANES).astype(BF16)
    g1_b = _pad_cols(gate_g1, LANES).astype(BF16)
    w2_b, a2_b = _pad_rows(decay_w2, LANES).astype(BF16), _pad_rows(aaa_a2, LANES).astype(BF16)
    g2_b = _pad_rows(gate_g2, LANES).astype(BF16)

    cos_p, sin_p = _rope_tables(jnp.arange(Tp, dtype=jnp.int32), head_dim)
    cos_s, sin_s = _rope_tables(past_len + jnp.arange(Tsp, dtype=jnp.int32), head_dim)

    x_s = jnp.pad(x_sample, ((0, 0), (0, Tsp - Ts), (0, 0)))
    groups = [dict(x=x_prompt.reshape(Bp * Tp, D), B=Bp, T=Tp, t_real=Tp),
              dict(x=x_s.reshape(Bs * Tsp, D), B=Bs, T=Tsp, t_real=Ts)]
    outs = [dict(k=[], v=[], wkv=[], shift=[]) for _ in groups]
    Ms = Bs * Tsp
    cos_st, sin_st = jnp.tile(cos_s, (Bs, 1)), jnp.tile(sin_s, (Bs, 1))
    bf16_rows = 2 * SUBLANES

    for layer in range(depth):
        pvec = jnp.stack([k_k[layer], k_a[layer], r_k[layer].reshape(-1), ln_x_w[layer], ln_x_b[layer]]
                         + [jnp.zeros((rw,), F32)] * (SUBLANES - 5))
        sprevs = [jnp.zeros((Bp, 1, D), F32), state_shift[layer][:, None, :]]
        pro = [_prologue(grp["x"].reshape(grp["B"], grp["T"], D), sprev, norm_mix[layer][None], mu_wag[layer],
                         w1_b[layer], a1_b[layer], g1_b[layer], w2_b[layer], a2_b[layer], g2_b[layer],
                         decay_w0[layer][None], aaa_a0[layer][None], grp["t_real"])
               for grp, sprev in zip(groups, sprevs)]
        xn = [p[0].reshape(grp["B"] * grp["T"], D) for p, grp in zip(pro, groups)]

        rope = dict(extra_p=(cos_p, sin_p), extra_s=(cos_st, sin_st), rows_per_seq=Tp, head_dim=head_dim)
        q = _ws_matmul([xn[0]], [xn[1]], [w_in], layer, 0, att_w, "rope", F32, **rope)
        k = _ws_matmul([xn[0]], [xn[1]], [w_in], layer, att_w, att_w, "rope", F32, **rope)
        v = _ws_matmul([xn[0]], [xn[1]], [w_in], layer, 2 * att_w, att_w, "plain", F32)
        shift_rows = _pad_rows(state_shift[layer], bf16_rows).astype(BF16)
        rkv = _ws_matmul([xn[0]], [jnp.concatenate([xn[1], shift_rows], axis=0)], [w_in], layer,
                         3 * att_w, 3 * rw, "plain", F32)
        y_rkv = [rkv[0], rkv[1][:Ms]]
        y_prev = [jnp.zeros((Bp, 1, 3 * rw), F32), rkv[1][Ms:Ms + Bs].reshape(Bs, 1, 3 * rw)]
        state0 = [None, state_wkv[layer]]

        o_att, o_rwkv = [], []
        for gi, (grp, out) in enumerate(zip(groups, outs)):
            B, T, tr = grp["B"], grp["T"], grp["t_real"]
            q3, k3, v3 = (z[gi].reshape(B, T, att_w) for z in (q, k, v))
            if gi == 0:
                o_att.append(_moba_prompt(q3, k3, v3, head_dim))
            else:
                o_att.append(_moba_sample_gather(q3, k3, v3, cache_k, cache_v, page_table, layer, tr))
            _, lw, a, g, shift = pro[gi]
            o, wkv_new = _wkv(y_rkv[gi].reshape(B, T, 3 * rw), y_prev[gi], lw, a, g, mu_rkv[layer], pvec,
                              state0[gi], tr)
            o_rwkv.append(o)
            out["k"].append(k3[:, :tr].reshape(B, tr, n_att_heads, head_dim))
            out["v"].append(v3[:, :tr].reshape(B, tr, n_att_heads, head_dim))
            out["wkv"].append(wkv_new)
            out["shift"].append(shift.reshape(B, D))

        rows = lambda z, grp: z.reshape(grp["B"] * grp["T"], -1)
        h = _ws_matmul([rows(o_att[0], groups[0]), rows(o_rwkv[0], groups[0])],
                       [rows(o_att[1], groups[1]), rows(o_rwkv[1], groups[1])],
                       [w_out], layer, 0, D, "residual", F32,
                       extra_p=(groups[0]["x"],), extra_s=(groups[1]["x"],))
        hn = [_rmsnorm(z, norm_ffn[layer][None], BF16) for z in h]
        act = _ws_matmul([hn[0]], [hn[1]], [ffn_w_gate, ffn_w_up], layer, 0, ffn_w_gate.shape[2], "swiglu", BF16)
        for gi, grp in enumerate(groups):
            grp["x"] = _matmul_residual([act[gi]], wd_b, layer, h[gi])

    ys = []
    for grp in groups:
        B, T = grp["B"], grp["T"]
        y = _rmsnorm(grp["x"], norm_final[None], F32).reshape(B, T, D)
        ys.append(y[:, :grp["t_real"]])
    po, so = outs
    return (ys[0], ys[1],
            jnp.stack(po["k"]), jnp.stack(po["v"]), jnp.stack(po["wkv"]), jnp.stack(po["shift"]),
            jnp.stack(so["k"]), jnp.stack(so["v"]), jnp.stack(so["wkv"]), jnp.stack(so["shift"]))
```
